```python
import math
import jax
import jax.numpy as jnp
from jax import lax
import numpy as np

D_MODEL = 2048
BATCH = 2
SEQ = 4096
DEPTH = 4
DEC_BATCH = 128
DEC_SEQ = 1
PAST_LEN = 8192
PAGE_SIZE = 128

MIX_WIDTH = D_MODEL
ATTN_WIDTH = MIX_WIDTH // 2
SSM_WIDTH = MIX_WIDTH - ATTN_WIDTH
N_HEADS = 8
V_DIM = ATTN_WIDTH // N_HEADS
NOPE_DIM = 128
ROPE_DIM = 64
QK_DIM = NOPE_DIM + ROPE_DIM
Q_LORA = D_MODEL // 4
KV_LORA = D_MODEL // 8
ROPE_THETA = 10000.0
Q_BLOCK = 128
SM_SCALE = QK_DIM ** -0.5
SSM_GROUP_CH = 16
SSM_GROUPS = SSM_WIDTH // SSM_GROUP_CH
SSM_STATE = 64
DT_MIN = 1e-3
DT_MAX = 1e-1
D_FF = 4 * D_MODEL
IN_WIDTH = Q_LORA + KV_LORA + ROPE_DIM + SSM_WIDTH
EPS = 1e-6

kernel_name = "hymba_mla_s5_sandwich_step"


def rms_norm(x, g):
    xf = x.astype(jnp.float32)
    y = xf * lax.rsqrt(jnp.mean(xf * xf, axis=-1, keepdims=True) + EPS)
    return (y * g.astype(jnp.float32)).astype(x.dtype)


def rope_tables(pos):
    inv_freq = jnp.exp(-math.log(ROPE_THETA) * 2.0 * jnp.arange(ROPE_DIM // 2, dtype=jnp.float32) / ROPE_DIM)
    ang = pos.astype(jnp.float32)[:, None] * inv_freq[None, :]
    return jnp.cos(ang), jnp.sin(ang)


def apply_rope(x, cos, sin):
    xf = x.astype(jnp.float32)
    x1, x2 = jnp.split(xf, 2, axis=-1)
    return jnp.concatenate([x1 * cos - x2 * sin, x1 * sin + x2 * cos], axis=-1).astype(x.dtype)


def mla_prompt(q_nope, q_rope, c_kv, k_rope, w_uk, w_uv):
    bsz, seq = q_nope.shape[:2]
    k_nope = jnp.einsum('bsc,chd->bshd', c_kv, w_uk)
    v = jnp.einsum('bsc,chd->bshd', c_kv, w_uv)
    nb = seq // Q_BLOCK
    qn_blk = q_nope.reshape(bsz, nb, Q_BLOCK, N_HEADS, NOPE_DIM).swapaxes(0, 1)
    qr_blk = q_rope.reshape(bsz, nb, Q_BLOCK, N_HEADS, ROPE_DIM).swapaxes(0, 1)
    key_pos = jnp.arange(seq)

    def one_block(args):
        qn, qr, i = args
        s = (jnp.einsum('bqhd,bkhd->bhqk', qn, k_nope).astype(jnp.float32)
             + jnp.einsum('bqhr,bkr->bhqk', qr, k_rope).astype(jnp.float32)) * SM_SCALE
        q_pos = i * Q_BLOCK + jnp.arange(Q_BLOCK)
        s = jnp.where(key_pos[None, :] <= q_pos[:, None], s, -jnp.inf)
        pr = jax.nn.softmax(s, axis=-1).astype(v.dtype)
        return jnp.einsum('bhqk,bkhd->bqhd', pr, v)

    out = lax.map(one_block, (qn_blk, qr_blk, jnp.arange(nb)))
    return out.swapaxes(0, 1).reshape(bsz, seq, N_HEADS * V_DIM)


def mla_sample(q_nope, q_rope, c_kv, k_rope, past_c, past_r, w_uk, w_uv):
    bsz, t = q_nope.shape[:2]
    past_len = past_c.shape[1]
    keys_c = jnp.concatenate([past_c, c_kv.astype(past_c.dtype)], axis=1)
    keys_r = jnp.concatenate([past_r, k_rope.astype(past_r.dtype)], axis=1)
    q_abs = jnp.einsum('bqhd,chd->bqhc', q_nope, w_uk)
    s = (jnp.einsum('bqhc,bkc->bhqk', q_abs, keys_c).astype(jnp.float32)
         + jnp.einsum('bqhr,bkr->bhqk', q_rope, keys_r).astype(jnp.float32)) * SM_SCALE
    mask = jnp.arange(past_len + t)[None, :] <= (past_len + jnp.arange(t))[:, None]
    s = jnp.where(mask, s, -jnp.inf)
    pr = jax.nn.softmax(s, axis=-1).astype(keys_c.dtype)
    o_lat = jnp.einsum('bhqk,bkc->bqhc', pr, keys_c)
    o = jnp.einsum('bqhc,chd->bqhd', o_lat, w_uv)
    return o.reshape(bsz, t, N_HEADS * V_DIM)


def s5_branch(u, p, h0_re=None, h0_im=None):
    bsz, seq, _ = u.shape
    f32 = jnp.float32
    uf = u.astype(f32).reshape(bsz, seq, SSM_GROUPS, SSM_GROUP_CH)
    dt = jnp.exp(p['log_dt'].astype(f32))[:, None]
    lr = p['lam_re'].astype(f32)
    li = p['lam_im'].astype(f32)
    mag = jnp.exp(lr * dt)
    ab_re = mag * jnp.cos(li * dt)
    ab_im = mag * jnp.sin(li * dt)
    den = lr * lr + li * li
    nr = ab_re - 1.0
    f_re = (nr * lr + ab_im * li) / den
    f_im = (ab_im * lr - nr * li) / den
    b_re = p['b_re'].astype(f32)
    b_im = p['b_im'].astype(f32)
    bb_re = f_re[..., None] * b_re - f_im[..., None] * b_im
    bb_im = f_re[..., None] * b_im + f_im[..., None] * b_re
    x_re = jnp.einsum('gph,bsgh->bsgp', bb_re, uf)
    x_im = jnp.einsum('gph,bsgh->bsgp', bb_im, uf)
    a_re = jnp.broadcast_to(ab_re, x_re.shape)
    a_im = jnp.broadcast_to(ab_im, x_im.shape)

    def combine(e1, e2):
        a1r, a1i, x1r, x1i = e1
        a2r, a2i, x2r, x2i = e2
        return (a2r * a1r - a2i * a1i, a2r * a1i + a2i * a1r,
                a2r * x1r - a2i * x1i + x2r, a2r * x1i + a2i * x1r + x2i)

    pw_re, pw_im, h_re, h_im = lax.associative_scan(combine, (a_re, a_im, x_re, x_im), axis=1)
    if h0_re is not None:
        s_re = h0_re.astype(f32)[:, None]
        s_im = h0_im.astype(f32)[:, None]
        h_re, h_im = (h_re + pw_re * s_re - pw_im * s_im,
                      h_im + pw_re * s_im + pw_im * s_re)
    y = (jnp.einsum('ghp,bsgp->bsgh', p['c_re'].astype(f32), h_re)
         - jnp.einsum('ghp,bsgp->bsgh', p['c_im'].astype(f32), h_im)
         + p['d_skip'].astype(f32) * uf)
    y = jax.nn.gelu(y).reshape(bsz, seq, SSM_WIDTH).astype(u.dtype)
    z = y @ p['w_glu']
    out = z[..., :SSM_WIDTH] * jax.nn.sigmoid(z[..., SSM_WIDTH:])
    return out, h_re[:, -1], h_im[:, -1]


def trunk_layer(x, pos, p, past_c=None, past_r=None, h0_re=None, h0_im=None):
    bsz, seq, _ = x.shape
    h = rms_norm(x, p['g_pre_mix'])
    z = h @ p['w_in']
    q_lat, kv_lat, k_rope, u = jnp.split(
        z, [Q_LORA, Q_LORA + KV_LORA, Q_LORA + KV_LORA + ROPE_DIM], axis=-1)
    q = (rms_norm(q_lat, p['g_q_lat']) @ p['w_uq']).reshape(bsz, seq, N_HEADS, QK_DIM)
    cos, sin = rope_tables(pos)
    q_nope = q[..., :NOPE_DIM]
    q_rope = apply_rope(q[..., NOPE_DIM:], cos[None, :, None], sin[None, :, None])
    c_kv = rms_norm(kv_lat, p['g_kv_lat'])
    k_rope = apply_rope(k_rope, cos[None], sin[None])
    if past_c is None:
        attn = mla_prompt(q_nope, q_rope, c_kv, k_rope, p['w_uk'], p['w_uv'])
    else:
        attn = mla_sample(q_nope, q_rope, c_kv, k_rope, past_c, past_r, p['w_uk'], p['w_uv'])
    ssm, h_re, h_im = s5_branch(u, p, h0_re, h0_im)
    mix = jnp.concatenate([rms_norm(attn, p['g_attn_out']), rms_norm(ssm, p['g_ssm_out'])],
                          axis=-1) @ p['w_out']
    x = x + rms_norm(mix, p['g_post_mix'])
    m = jnp.square(jax.nn.relu(rms_norm(x, p['g_pre_mlp']) @ p['w_up'])) @ p['w_down']
    x = x + rms_norm(m, p['g_post_mlp'])
    return x, c_kv, k_rope, h_re, h_im


def setup_inputs(seed: int = 0) -> dict:
    key = jax.random.key(seed)
    ks = iter(jax.random.split(key, 48))
    f32 = jnp.float32

    def nrm(shape, scale):
        return scale * jax.random.normal(next(ks), shape, f32)

    def gain(n):
        return 1.0 + nrm((DEPTH, n), 0.05)

    n_pages = PAST_LEN // PAGE_SIZE
    n_used = DEC_BATCH * n_pages
    n_pool = n_used + n_used // 4
    half = math.sqrt(0.5)
    x_prompt = nrm((BATCH, SEQ, D_MODEL), 1.0)
    x_sample = nrm((DEC_BATCH, DEC_SEQ, D_MODEL), 1.0)
    cache_kv_latent = nrm((DEPTH, n_pool, PAGE_SIZE, KV_LORA), 1.0)
    cache_k_rope = nrm((DEPTH, n_pool, PAGE_SIZE, ROPE_DIM), 1.0)
    state_ssm_re = nrm((DEPTH, DEC_BATCH, SSM_GROUPS, SSM_STATE), 0.5)
    state_ssm_im = nrm((DEPTH, DEC_BATCH, SSM_GROUPS, SSM_STATE), 0.5)
    page_table = jax.random.permutation(next(ks), n_pool)[:n_used].reshape(
        DEC_BATCH, n_pages).astype(jnp.int32)
    lam_re = -0.5 + nrm((DEPTH, SSM_GROUPS, SSM_STATE), 0.01)
    lam_im = jnp.pi * jnp.arange(SSM_STATE, dtype=f32)[None, None, :] + nrm((DEPTH, SSM_GROUPS, SSM_STATE), 0.01)
    log_dt = jax.random.uniform(next(ks), (DEPTH, SSM_GROUPS), f32, math.log(DT_MIN), math.log(DT_MAX))
    return {
        'x_prompt': x_prompt,
        'x_sample': x_sample,
        'cache_kv_latent': cache_kv_latent,
        'cache_k_rope': cache_k_rope,
        'state_ssm_re': state_ssm_re,
        'state_ssm_im': state_ssm_im,
        'page_table': page_table,
        'g_pre_mix': gain(D_MODEL),
        'w_in': nrm((DEPTH, D_MODEL, IN_WIDTH), D_MODEL ** -0.5),
        'g_q_lat': gain(Q_LORA),
        'w_uq': nrm((DEPTH, Q_LORA, N_HEADS * QK_DIM), Q_LORA ** -0.5),
        'g_kv_lat': gain(KV_LORA),
        'w_uk': nrm((DEPTH, KV_LORA, N_HEADS, NOPE_DIM), KV_LORA ** -0.5),
        'w_uv': nrm((DEPTH, KV_LORA, N_HEADS, V_DIM), KV_LORA ** -0.5),
        'lam_re': lam_re,
        'lam_im': lam_im,
        'log_dt': log_dt,
        'b_re': nrm((DEPTH, SSM_GROUPS, SSM_STATE, SSM_GROUP_CH), half * SSM_GROUP_CH ** -0.5),
        'b_im': nrm((DEPTH, SSM_GROUPS, SSM_STATE, SSM_GROUP_CH), half * SSM_GROUP_CH ** -0.5),
        'c_re': nrm((DEPTH, SSM_GROUPS, SSM_GROUP_CH, SSM_STATE), half * SSM_STATE ** -0.5),
        'c_im': nrm((DEPTH, SSM_GROUPS, SSM_GROUP_CH, SSM_STATE), half * SSM_STATE ** -0.5),
        'd_skip': nrm((DEPTH, SSM_GROUPS, SSM_GROUP_CH), 1.0),
        'w_glu': nrm((DEPTH, SSM_WIDTH, 2 * SSM_WIDTH), SSM_WIDTH ** -0.5),
        'g_attn_out': gain(ATTN_WIDTH),
        'g_ssm_out': gain(SSM_WIDTH),
        'w_out': nrm((DEPTH, MIX_WIDTH, D_MODEL), MIX_WIDTH ** -0.5),
        'g_post_mix': gain(D_MODEL),
        'g_pre_mlp': gain(D_MODEL),
        'w_up': nrm((DEPTH, D_MODEL, D_FF), D_MODEL ** -0.5),
        'w_down': nrm((DEPTH, D_FF, D_MODEL), D_FF ** -0.5),
        'g_post_mlp': gain(D_MODEL),
    }


def reference(x_prompt, x_sample, cache_kv_latent, cache_k_rope, state_ssm_re, state_ssm_im,
              page_table, g_pre_mix, w_in, g_q_lat, w_uq, g_kv_lat, w_uk, w_uv,
              lam_re, lam_im, log_dt, b_re, b_im, c_re, c_im, d_skip, w_glu,
              g_attn_out, g_ssm_out, w_out, g_post_mix, g_pre_mlp, w_up, w_down, g_post_mlp):
    n_seq = page_table.shape[0]
    past_len = page_table.shape[1] * cache_kv_latent.shape[2]
    pos_p = jnp.arange(x_prompt.shape[1], dtype=jnp.int32)
    pos_s = past_len + jnp.arange(x_sample.shape[1], dtype=jnp.int32)
    yp, ys = x_prompt, x_sample
    kv_p, kr_p, sr_p, si_p = [], [], [], []
    kv_s, kr_s, sr_s, si_s = [], [], [], []
    for l in range(DEPTH):
        p = dict(g_pre_mix=g_pre_mix[l], w_in=w_in[l], g_q_lat=g_q_lat[l], w_uq=w_uq[l],
                 g_kv_lat=g_kv_lat[l], w_uk=w_uk[l], w_uv=w_uv[l], lam_re=lam_re[l],
                 lam_im=lam_im[l], log_dt=log_dt[l], b_re=b_re[l], b_im=b_im[l],
                 c_re=c_re[l], c_im=c_im[l], d_skip=d_skip[l], w_glu=w_glu[l],
                 g_attn_out=g_attn_out[l], g_ssm_out=g_ssm_out[l], w_out=w_out[l],
                 g_post_mix=g_post_mix[l], g_pre_mlp=g_pre_mlp[l], w_up=w_up[l],
                 w_down=w_down[l], g_post_mlp=g_post_mlp[l])
        yp, c_new, r_new, h_re, h_im = trunk_layer(yp, pos_p, p)
        kv_p.append(c_new)
        kr_p.append(r_new)
        sr_p.append(h_re)
        si_p.append(h_im)
        past_c = cache_kv_latent[l][page_table].reshape(n_seq, past_len, KV_LORA)
        past_r = cache_k_rope[l][page_table].reshape(n_seq, past_len, ROPE_DIM)
        ys, c_new, r_new, h_re, h_im = trunk_layer(ys, pos_s, p, past_c, past_r,
                                                   state_ssm_re[l], state_ssm_im[l])
        kv_s.append(c_new)
        kr_s.append(r_new)
        sr_s.append(h_re)
        si_s.append(h_im)
    return (yp, ys, jnp.stack(kv_p), jnp.stack(kr_p), jnp.stack(sr_p), jnp.stack(si_p),
            jnp.stack(kv_s), jnp.stack(kr_s), jnp.stack(sr_s), jnp.stack(si_s))
```

```python
import functools
import math

import jax
import jax.numpy as jnp
from jax import lax
from jax.experimental import pallas as pl
from jax.experimental.pallas import tpu as pltpu

F32 = jnp.float32
BF16 = jnp.bfloat16

EPS = 1e-6
ROPE_THETA = 10000.0
LANE = 128
SUBLANE = 8
MXU_DIM = 256
VMEM_LIMIT = 56 * 1024 * 1024
N_SEG = SUBLANE


def _params(n_axes):
    return pltpu.CompilerParams(dimension_semantics=("arbitrary",) * n_axes,
                                vmem_limit_bytes=VMEM_LIMIT)


def _const_spec(shape):
    zeros = (0,) * len(shape)
    return pl.BlockSpec(shape, lambda *_: zeros)


def _rms(x, g):
    return x * lax.rsqrt(jnp.mean(x * x, axis=-1, keepdims=True) + EPS) * g


def _dot(a, b):
    return jnp.dot(a, b, preferred_element_type=F32)


def _dot_nt(a, b):
    return lax.dot_general(a, b, (((1,), (1,)), ((), ())), preferred_element_type=F32)


def _in_proj_kernel(x_ref, g_ref, w_ref, gq_ref, gkv_ref, cos_ref, sin_ref,
                    qlat_ref, ckv_ref, ckvb_ref, kr_ref, krb_ref, u_ref,
                    *, q_lora, kv_lora, ssm_w, rope):
    h = _rms(x_ref[...], g_ref[...]).astype(BF16)
    z = _dot(h, w_ref[...])
    o_kv = q_lora
    o_u = o_kv + kv_lora
    o_r = o_u + ssm_w
    qlat_ref[...] = _rms(z[:, :o_kv], gq_ref[...]).astype(BF16)
    ckv = _rms(z[:, o_kv:o_u], gkv_ref[...])
    ckv_ref[...] = ckv
    ckvb_ref[...] = ckv.astype(BF16)
    u_ref[...] = z[:, o_u:o_r]
    rot = z[:, o_r:o_r + LANE] * cos_ref[...] + z[:, o_r + LANE:o_r + 2 * LANE] * sin_ref[...]
    kr_ref[...] = rot[:, :rope]
    krb_ref[...] = rot.astype(BF16)


def _in_proj(x, g, w, gq, gkv, cos, sin, *, tm, q_lora, kv_lora, ssm_w, rope):
    t, d = x.shape
    n = w.shape[1]
    n_pos_blk = cos.shape[0] // tm
    row = lambda i: (i, 0)
    pos = lambda i: (i % n_pos_blk, 0)
    return pl.pallas_call(
        functools.partial(_in_proj_kernel, q_lora=q_lora, kv_lora=kv_lora, ssm_w=ssm_w, rope=rope),
        grid=(t // tm,),
        in_specs=[pl.BlockSpec((tm, d), row), _const_spec((1, d)), _const_spec((d, n)),
                  _const_spec((1, q_lora)), _const_spec((1, kv_lora)),
                  pl.BlockSpec((tm, LANE), pos), pl.BlockSpec((tm, LANE), pos)],
        out_specs=[pl.BlockSpec((tm, q_lora), row), pl.BlockSpec((tm, kv_lora), row),
                   pl.BlockSpec((tm, kv_lora), row), pl.BlockSpec((tm, rope), row),
                   pl.BlockSpec((tm, LANE), row), pl.BlockSpec((tm, ssm_w), row)],
        out_shape=[jax.ShapeDtypeStruct((t, q_lora), BF16), jax.ShapeDtypeStruct((t, kv_lora), F32),
                   jax.ShapeDtypeStruct((t, kv_lora), BF16), jax.ShapeDtypeStruct((t, rope), F32),
                   jax.ShapeDtypeStruct((t, LANE), BF16), jax.ShapeDtypeStruct((t, ssm_w), F32)],
        compiler_params=_params(1),
        name="in_proj",
    )(x, g, w, gq, gkv, cos, sin)


def _qkv_prompt_kernel(qlat_ref, ckvb_ref, krb_ref, wuq_ref, wuk_ref, wuv_ref, cos_ref, sin_ref,
                       q_ref, k_ref, v_ref, *, n_heads, scale):
    z = _dot(qlat_ref[...], wuq_ref[...])
    ckvb = ckvb_ref[...]
    kn = _dot(ckvb, wuk_ref[...])
    v_ref[...] = _dot(ckvb, wuv_ref[...]).astype(BF16)
    cos = cos_ref[...]
    sin = sin_ref[...]
    krb = krb_ref[...]
    o_r = n_heads * LANE
    o_s = 2 * n_heads * LANE
    for h in range(n_heads):
        lo = h * LANE
        hd = h * MXU_DIM
        q_ref[:, hd:hd + LANE] = (z[:, lo:lo + LANE] * scale).astype(BF16)
        rot = z[:, o_r + lo:o_r + lo + LANE] * cos + z[:, o_s + lo:o_s + lo + LANE] * sin
        q_ref[:, hd + LANE:hd + 2 * LANE] = (rot * scale).astype(BF16)
        k_ref[:, hd:hd + LANE] = kn[:, lo:lo + LANE].astype(BF16)
        k_ref[:, hd + LANE:hd + 2 * LANE] = krb


def _qkv_prompt(qlat, ckvb, krb, wuq, wuk, wuv, cos, sin, *, tm, n_heads, scale):
    t, q_lora = qlat.shape
    kv_lora = ckvb.shape[1]
    n_pos_blk = cos.shape[0] // tm
    row = lambda i: (i, 0)
    pos = lambda i: (i % n_pos_blk, 0)
    hw = n_heads * MXU_DIM
    vw = wuv.shape[1]
    return pl.pallas_call(
        functools.partial(_qkv_prompt_kernel, n_heads=n_heads, scale=scale),
        grid=(t // tm,),
        in_specs=[pl.BlockSpec((tm, q_lora), row), pl.BlockSpec((tm, kv_lora), row),
                  pl.BlockSpec((tm, LANE), row), _const_spec(wuq.shape), _const_spec(wuk.shape),
                  _const_spec(wuv.shape), pl.BlockSpec((tm, LANE), pos), pl.BlockSpec((tm, LANE), pos)],
        out_specs=[pl.BlockSpec((tm, hw), row), pl.BlockSpec((tm, hw), row), pl.BlockSpec((tm, vw), row)],
        out_shape=[jax.ShapeDtypeStruct((t, hw), BF16), jax.ShapeDtypeStruct((t, hw), BF16),
                   jax.ShapeDtypeStruct((t, vw), BF16)],
        compiler_params=_params(1),
        name="qkv_prompt",
    )(qlat, ckvb, krb, wuq, wuk, wuv, cos, sin)


def _q_sample_kernel(qlat_ref, wuq_ref, wuk_ref, cos_ref, sin_ref, q_ref, *, n_heads, kv_lora, scale):
    z = _dot(qlat_ref[...], wuq_ref[...])
    cos = cos_ref[...]
    sin = sin_ref[...]
    o_r = n_heads * LANE
    o_s = 2 * n_heads * LANE
    hw = kv_lora + LANE
    for h in range(n_heads):
        lo = h * LANE
        qn = z[:, lo:lo + LANE].astype(BF16)
        q_abs = _dot_nt(qn, wuk_ref[:, lo:lo + LANE])
        q_ref[:, h * hw:h * hw + kv_lora] = (q_abs * scale).astype(BF16)
        rot = z[:, o_r + lo:o_r + lo + LANE] * cos + z[:, o_s + lo:o_s + lo + LANE] * sin
        q_ref[:, h * hw + kv_lora:(h + 1) * hw] = (rot * scale).astype(BF16)


def _q_sample(qlat, wuq, wuk, cos, sin, *, n_heads, scale):
    t = qlat.shape[0]
    kv_lora = wuk.shape[0]
    hw = kv_lora + LANE
    return pl.pallas_call(
        functools.partial(_q_sample_kernel, n_heads=n_heads, kv_lora=kv_lora, scale=scale),
        grid=(1,),
        in_specs=[_const_spec(qlat.shape), _const_spec(wuq.shape), _const_spec(wuk.shape),
                  _const_spec(cos.shape), _const_spec(sin.shape)],
        out_specs=_const_spec((t, n_heads * hw)),
        out_shape=jax.ShapeDtypeStruct((t, n_heads * hw), BF16),
        compiler_params=_params(1),
        name="q_sample",
    )(qlat, wuq, wuk, cos, sin)


def _flash_kernel(q_ref, k_ref, v_ref, o_ref, m_ref, l_ref, acc_ref, *, blk):
    i = pl.program_id(2)
    q = q_ref[...]
    m_ref[...] = jnp.full(m_ref.shape, -jnp.inf, F32)
    l_ref[...] = jnp.zeros(l_ref.shape, F32)
    acc_ref[...] = jnp.zeros(acc_ref.shape, F32)

    def update(s, v):
        m_prev = m_ref[...]
        m_new = jnp.maximum(m_prev, jnp.max(s, axis=-1, keepdims=True))
        alpha = jnp.exp(m_prev - m_new)
        p = jnp.exp(s - m_new)
        l_ref[...] = alpha * l_ref[...] + jnp.sum(p, axis=-1, keepdims=True)
        acc_ref[...] = alpha * acc_ref[...] + _dot(p.astype(BF16), v)
        m_ref[...] = m_new

    def below_diagonal(j, carry):
        ks = pl.multiple_of(j * blk, blk)
        update(_dot_nt(q, k_ref[pl.ds(ks, blk), :]), v_ref[pl.ds(ks, blk), :])
        return carry

    lax.fori_loop(0, i, below_diagonal, 0)

    ks = pl.multiple_of(i * blk, blk)
    s = _dot_nt(q, k_ref[pl.ds(ks, blk), :])
    rows = lax.broadcasted_iota(jnp.int32, s.shape, 0)
    cols = lax.broadcasted_iota(jnp.int32, s.shape, 1)
    update(jnp.where(cols <= rows, s, -jnp.inf), v_ref[pl.ds(ks, blk), :])
    o_ref[...] = acc_ref[...] / l_ref[...]


def _flash(q, k, v, *, bsz, seq, n_heads, blk):
    v_dim = v.shape[1] // n_heads
    nq = seq // blk
    return pl.pallas_call(
        functools.partial(_flash_kernel, blk=blk),
        grid=(bsz, n_heads, nq),
        in_specs=[pl.BlockSpec((blk, MXU_DIM), lambda b, h, i: (b * nq + i, h)),
                  pl.BlockSpec((seq, MXU_DIM), lambda b, h, i: (b, h)),
                  pl.BlockSpec((seq, v_dim), lambda b, h, i: (b, h))],
        out_specs=pl.BlockSpec((blk, v_dim), lambda b, h, i: (b * nq + i, h)),
        out_shape=jax.ShapeDtypeStruct((bsz * seq, n_heads * v_dim), F32),
        scratch_shapes=[pltpu.VMEM((blk, 1), F32), pltpu.VMEM((blk, 1), F32),
                        pltpu.VMEM((blk, v_dim), F32)],
        compiler_params=_params(3),
        name="flash_prompt",
    )(q, k, v)


PAGES_PER_STEP = 16


def _paged_kernel(pt_ref, q_ref, cnew_ref, rnew_ref, *rest, n_pg, kv_lora, rope):
    del pt_ref
    c_refs = rest[:n_pg]
    r_refs = rest[n_pg:2 * n_pg]
    o_ref, m_ref, l_ref, acc_ref = rest[2 * n_pg:]
    j = pl.program_id(1)

    @pl.when(j == 0)
    def _():
        m_ref[...] = jnp.full(m_ref.shape, -jnp.inf, F32)
        l_ref[...] = jnp.zeros(l_ref.shape, F32)
        acc_ref[...] = jnp.zeros(acc_ref.shape, F32)

    q = q_ref[0]
    qc = q[:, :kv_lora]
    qr = q[:, kv_lora:kv_lora + rope]
    cs = [c_refs[p][0, 0].astype(BF16) for p in range(n_pg)]
    s = jnp.concatenate(
        [_dot_nt(qc, cs[p]) + _dot_nt(qr, r_refs[p][0, 0].astype(BF16)) for p in range(n_pg)], axis=-1)
    m_prev = m_ref[...]
    m_new = jnp.maximum(m_prev, jnp.max(s, axis=-1, keepdims=True))
    alpha = jnp.exp(m_prev - m_new)
    p_all = jnp.exp(s - m_new)
    l_ref[...] = alpha * l_ref[...] + jnp.sum(p_all, axis=-1, keepdims=True)
    page = cs[0].shape[0]
    pv = _dot(p_all[:, :page].astype(BF16), cs[0])
    for p in range(1, n_pg):
        pv = pv + _dot(p_all[:, p * page:(p + 1) * page].astype(BF16), cs[p])
    acc_ref[...] = alpha * acc_ref[...] + pv
    m_ref[...] = m_new

    @pl.when(j == pl.num_programs(1) - 1)
    def _():
        cn = cnew_ref[0]
        qf = q.astype(F32)
        s_new = (jnp.sum(qf[:, :kv_lora] * cn, axis=-1, keepdims=True)
                 + jnp.sum(qf[:, kv_lora:] * rnew_ref[0], axis=-1, keepdims=True))
        m_old = m_ref[...]
        m_fin = jnp.maximum(m_old, s_new)
        a = jnp.exp(m_old - m_fin)
        p_new = jnp.exp(s_new - m_fin)
        o_ref[0] = (a * acc_ref[...] + p_new * cn) / (a * l_ref[...] + p_new)


def _paged(page_table, q, cnew, rnew, cache_c, cache_r, *, layer, n_heads):
    n_seq, n_pages = page_table.shape
    _, _, page, kv_lora = cache_c.shape
    rope = cache_r.shape[3]
    hw = q.shape[2]
    n_pg = PAGES_PER_STEP

    def page_spec(width, p):
        return pl.BlockSpec((1, 1, page, width),
                            lambda b, j, pt: (layer, pt[b, j * n_pg + p], 0, 0))

    seq3 = lambda b, j, pt: (b, 0, 0)
    grid_spec = pltpu.PrefetchScalarGridSpec(
        num_scalar_prefetch=1,
        grid=(n_seq, n_pages // n_pg),
        in_specs=([pl.BlockSpec((1, n_heads, hw), seq3), pl.BlockSpec((1, 1, kv_lora), seq3),
                   pl.BlockSpec((1, 1, LANE), seq3)]
                  + [page_spec(kv_lora, p) for p in range(n_pg)]
                  + [page_spec(rope, p) for p in range(n_pg)]),
        out_specs=pl.BlockSpec((1, n_heads, kv_lora), seq3),
        scratch_shapes=[pltpu.VMEM((n_heads, 1), F32), pltpu.VMEM((n_heads, 1), F32),
                        pltpu.VMEM((n_heads, kv_lora), F32)],
    )
    return pl.pallas_call(
        functools.partial(_paged_kernel, n_pg=n_pg, kv_lora=kv_lora, rope=rope),
        grid_spec=grid_spec,
        out_shape=jax.ShapeDtypeStruct((n_seq, n_heads, kv_lora), F32),
        compiler_params=_params(2),
        name="paged_sample",
    )(page_table, q, cnew, rnew, *([cache_c] * n_pg), *([cache_r] * n_pg))


def _uv_kernel(ol_ref, wuv_ref, o_ref, *, n_heads, kv_lora):
    for h in range(n_heads):
        o_ref[:, h * LANE:(h + 1) * LANE] = _dot(
            ol_ref[:, h * kv_lora:(h + 1) * kv_lora].astype(BF16), wuv_ref[:, h * LANE:(h + 1) * LANE])


def _uv_proj(o_lat, wuv, *, n_heads):
    t = o_lat.shape[0]
    kv_lora = wuv.shape[0]
    return pl.pallas_call(
        functools.partial(_uv_kernel, n_heads=n_heads, kv_lora=kv_lora),
        grid=(1,),
        in_specs=[_const_spec(o_lat.shape), _const_spec(wuv.shape)],
        out_specs=_const_spec((t, wuv.shape[1])),
        out_shape=jax.ShapeDtypeStruct((t, wuv.shape[1]), F32),
        compiler_params=_params(1),
        name="uv_sample",
    )(o_lat, wuv)


def _s5_disc_kernel(lr_ref, li_ref, ldt_ref, bre_ref, bim_ref,
                    abr_ref, abi_ref, apr_ref, api_ref, bbr_ref, bbi_ref, *, n_sq):
    lr = lr_ref[...]
    li = li_ref[...]
    dt = jnp.exp(ldt_ref[...])
    mag = jnp.exp(lr * dt)
    ab_re = mag * jnp.cos(li * dt)
    ab_im = mag * jnp.sin(li * dt)
    den = lr * lr + li * li
    nr = ab_re - 1.0
    f_re = (nr * lr + ab_im * li) / den
    f_im = (ab_im * lr - nr * li) / den
    abr_ref[...] = ab_re
    abi_ref[...] = ab_im
    pr, pi = ab_re, ab_im
    for _ in range(n_sq):
        pr, pi = pr * pr - pi * pi, 2.0 * pr * pi
    apr_ref[...] = pr
    api_ref[...] = pi
    for h in range(bre_ref.shape[0]):
        b_re = bre_ref[h]
        b_im = bim_ref[h]
        bbr_ref[h] = f_re * b_re - f_im * b_im
        bbi_ref[h] = f_re * b_im + f_im * b_re


def _s5_disc(lam_re, lam_im, log_dt, b_re, b_im, *, seg_len):
    g, p = lam_re.shape
    ch = b_re.shape[2]
    n_sq = int(math.log2(seg_len))
    assert 2 ** n_sq == seg_len
    b_re_t = jnp.transpose(b_re, (2, 0, 1))
    b_im_t = jnp.transpose(b_im, (2, 0, 1))
    gp = jax.ShapeDtypeStruct((g, p), F32)
    hgp = jax.ShapeDtypeStruct((ch, g, p), F32)
    return pl.pallas_call(
        functools.partial(_s5_disc_kernel, n_sq=n_sq),
        grid=(1,),
        in_specs=[_const_spec((g, p)), _const_spec((g, p)), _const_spec((g, 1)),
                  _const_spec((ch, g, p)), _const_spec((ch, g, p))],
        out_specs=[_const_spec((g, p))] * 4 + [_const_spec((ch, g, p))] * 2,
        out_shape=[gp, gp, gp, gp, hgp, hgp],
        compiler_params=_params(1),
        name="s5_disc",
    )(lam_re, lam_im, log_dt.reshape(g, 1), b_re_t, b_im_t)


def _s5_dense(bb_re, bb_im, c_re, c_im):
    ch, g, p = bb_re.shape
    gk = MXU_DIM // ch
    go = LANE // ch

    def in_side(bb):
        m = jnp.transpose(bb, (1, 0, 2)).reshape(g // gk, gk, ch, p)
        eye = jnp.eye(gk, dtype=F32)
        return jnp.einsum('kghp,gj->kghjp', m, eye).reshape(g // gk, gk * ch, gk * p).astype(BF16)

    def out_side(c):
        m = jnp.transpose(c, (0, 2, 1)).reshape(g // go, go, p, ch)
        eye = jnp.eye(go, dtype=F32)
        return jnp.einsum('kgph,gj->kgpjh', m, eye).reshape(g // go, go * p, go * ch).astype(BF16)

    return in_side(bb_re), in_side(bb_im), out_side(c_re), out_side(-c_im)


def _s5_readout(hr, hi, u, cdr_ref, cdi_ref, d_ref, wglu_ref):
    n_ob, kc, _ = cdr_ref.shape
    hrb = hr.astype(BF16)
    hib = hi.astype(BF16)
    y = jnp.concatenate(
        [_dot(hrb[:, ob * kc:(ob + 1) * kc], cdr_ref[ob]) + _dot(hib[:, ob * kc:(ob + 1) * kc], cdi_ref[ob])
         for ob in range(n_ob)], axis=-1)
    y = jax.nn.gelu(y + d_ref[...] * u).astype(BF16)
    z = _dot(y, wglu_ref[...])
    w = z.shape[1] // 2
    return z[:, :w] * jax.nn.sigmoid(z[:, w:])


def _s5_project_in(lhs, wbr_ref, wbi_ref, xr_ref, xi_ref):
    n_kt, kc, nc = wbr_ref.shape
    for kt in range(n_kt):
        piece = lhs[:, kt * kc:(kt + 1) * kc]
        xr_ref[:, kt * nc:(kt + 1) * nc] = _dot(piece, wbr_ref[kt])
        xi_ref[:, kt * nc:(kt + 1) * nc] = _dot(piece, wbi_ref[kt])


def _s5_prompt_kernel(u_ref, wbr_ref, wbi_ref, ar_ref, ai_ref, apr_ref, api_ref,
                      cdr_ref, cdi_ref, d_ref, wglu_ref,
                      ssm_ref, hfr_ref, hfi_ref,
                      uperm_ref, xr_ref, xi_ref, hr_ref, hi_ref, *, sb):
    ps = pl.program_id(1)
    blk = pl.program_id(2)

    @pl.when((ps == 0) & (blk == 0))
    def _():
        hr_ref[...] = jnp.zeros(hr_ref.shape, F32)
        hi_ref[...] = jnp.zeros(hi_ref.shape, F32)

    @pl.when((ps == 1) & (blk == 0))
    def _():
        apr = apr_ref[...]
        api = api_ref[...]
        in_r = jnp.zeros(apr.shape, F32)
        in_i = jnp.zeros(apr.shape, F32)
        for seg in range(N_SEG):
            loc_r = hr_ref[seg:seg + 1, :]
            loc_i = hi_ref[seg:seg + 1, :]
            hr_ref[seg:seg + 1, :] = in_r
            hi_ref[seg:seg + 1, :] = in_i
            in_r, in_i = (loc_r + apr * in_r - api * in_i, loc_i + apr * in_i + api * in_r)

    for s in range(sb):
        uperm_ref[s * N_SEG:(s + 1) * N_SEG, :] = u_ref[0, :, s, :]
    _s5_project_in(uperm_ref[...].astype(BF16), wbr_ref, wbi_ref, xr_ref, xi_ref)

    def step(s, carry):
        r0 = pl.multiple_of(s * N_SEG, N_SEG)
        hr = hr_ref[...]
        hi = hi_ref[...]
        ar = ar_ref[...]
        ai = ai_ref[...]
        nr = ar * hr - ai * hi + xr_ref[pl.ds(r0, N_SEG), :]
        ni = ar * hi + ai * hr + xi_ref[pl.ds(r0, N_SEG), :]
        hr_ref[...] = nr
        hi_ref[...] = ni
        xr_ref[pl.ds(r0, N_SEG), :] = nr
        xi_ref[pl.ds(r0, N_SEG), :] = ni
        return carry

    lax.fori_loop(0, sb, step, 0)

    @pl.when(ps == 1)
    def _():
        out = _s5_readout(xr_ref[...], xi_ref[...], uperm_ref[...], cdr_ref, cdi_ref, d_ref, wglu_ref)
        for s in range(sb):
            ssm_ref[0, :, s, :] = out[s * N_SEG:(s + 1) * N_SEG, :]
        hfr_ref[0] = hr_ref[...]
        hfi_ref[0] = hi_ref[...]


def _s5_prompt(u, wbr, wbi, ab_re, ab_im, ap_re, ap_im, cdr, cdi, d_skip, wglu, *, bsz, seq, sb):
    width = u.shape[1]
    gp = ab_re.size
    seg_len = seq // N_SEG
    nblk = seg_len // sb
    rows = sb * N_SEG
    u4 = u.reshape(bsz, N_SEG, seg_len, width)
    a_r = jnp.broadcast_to(ab_re.reshape(1, gp), (N_SEG, gp))
    a_i = jnp.broadcast_to(ab_im.reshape(1, gp), (N_SEG, gp))
    blk4 = lambda b, ps, k: (b, 0, k, 0)
    ssm, hf_r, hf_i = pl.pallas_call(
        functools.partial(_s5_prompt_kernel, sb=sb),
        grid=(bsz, 2, nblk),
        in_specs=[pl.BlockSpec((1, N_SEG, sb, width), blk4),
                  _const_spec(wbr.shape), _const_spec(wbi.shape),
                  _const_spec((N_SEG, gp)), _const_spec((N_SEG, gp)),
                  _const_spec((1, gp)), _const_spec((1, gp)),
                  _const_spec(cdr.shape), _const_spec(cdi.shape),
                  _const_spec((1, width)), _const_spec(wglu.shape)],
        out_specs=[pl.BlockSpec((1, N_SEG, sb, width), lambda b, ps, k: (b, 0, k * ps, 0)),
                   pl.BlockSpec((1, N_SEG, gp), lambda b, ps, k: (b, 0, 0)),
                   pl.BlockSpec((1, N_SEG, gp), lambda b, ps, k: (b, 0, 0))],
        out_shape=[jax.ShapeDtypeStruct((bsz, N_SEG, seg_len, width), F32),
                   jax.ShapeDtypeStruct((bsz, N_SEG, gp), F32),
                   jax.ShapeDtypeStruct((bsz, N_SEG, gp), F32)],
        scratch_shapes=[pltpu.VMEM((rows, width), F32),
                        pltpu.VMEM((rows, gp), F32), pltpu.VMEM((rows, gp), F32),
                        pltpu.VMEM((N_SEG, gp), F32), pltpu.VMEM((N_SEG, gp), F32)],
        compiler_params=_params(3),
        name="s5_prompt",
    )(u4, wbr, wbi, a_r, a_i, ap_re.reshape(1, gp), ap_im.reshape(1, gp), cdr, cdi,
      d_skip.reshape(1, width), wglu)
    return ssm.reshape(bsz * seq, width), hf_r[:, N_SEG - 1], hf_i[:, N_SEG - 1]


def _s5_sample_kernel(u_ref, h0r_ref, h0i_ref, wbr_ref, wbi_ref, ar_ref, ai_ref,
                      cdr_ref, cdi_ref, d_ref, wglu_ref,
                      ssm_ref, hr_ref, hi_ref, xr_ref, xi_ref):
    u = u_ref[...]
    _s5_project_in(u.astype(BF16), wbr_ref, wbi_ref, xr_ref, xi_ref)
    ar = ar_ref[...]
    ai = ai_ref[...]
    h0r = h0r_ref[...]
    h0i = h0i_ref[...]
    hr = ar * h0r - ai * h0i + xr_ref[...]
    hi = ar * h0i + ai * h0r + xi_ref[...]
    hr_ref[...] = hr
    hi_ref[...] = hi
    ssm_ref[...] = _s5_readout(hr, hi, u, cdr_ref, cdi_ref, d_ref, wglu_ref)


def _s5_sample(u, h0_re, h0_im, wbr, wbi, ab_re, ab_im, cdr, cdi, d_skip, wglu):
    t, width = u.shape
    gp = ab_re.size
    st = jax.ShapeDtypeStruct((t, gp), F32)
    args = (u, h0_re.reshape(t, gp), h0_im.reshape(t, gp), wbr, wbi,
            ab_re.reshape(1, gp), ab_im.reshape(1, gp), cdr, cdi, d_skip.reshape(1, width), wglu)
    return pl.pallas_call(
        _s5_sample_kernel,
        grid=(1,),
        in_specs=[_const_spec(a.shape) for a in args],
        out_specs=[_const_spec((t, width)), _const_spec((t, gp)), _const_spec((t, gp))],
        out_shape=[jax.ShapeDtypeStruct((t, width), F32), st, st],
        scratch_shapes=[pltpu.VMEM((t, gp), F32), pltpu.VMEM((t, gp), F32)],
        compiler_params=_params(1),
        name="s5_sample",
    )(*args)


def _out_proj_kernel(x_ref, attn_ref, ssm_ref, ga_ref, gs_ref, wa_ref, ws_ref, gp_ref, o_ref):
    na = _rms(attn_ref[...], ga_ref[...]).astype(BF16)
    ns = _rms(ssm_ref[...], gs_ref[...]).astype(BF16)
    mix = _dot(na, wa_ref[...]) + _dot(ns, ws_ref[...])
    o_ref[...] = x_ref[...] + _rms(mix, gp_ref[...])


def _out_proj(x, attn, ssm, ga, gs, wa, ws, gp, *, tm):
    t, d = x.shape
    aw = attn.shape[1]
    sw = ssm.shape[1]
    row = lambda i: (i, 0)
    return pl.pallas_call(
        _out_proj_kernel,
        grid=(t // tm,),
        in_specs=[pl.BlockSpec((tm, d), row), pl.BlockSpec((tm, aw), row), pl.BlockSpec((tm, sw), row),
                  _const_spec((1, aw)), _const_spec((1, sw)), _const_spec(wa.shape), _const_spec(ws.shape),
                  _const_spec((1, d))],
        out_specs=pl.BlockSpec((tm, d), row),
        out_shape=jax.ShapeDtypeStruct((t, d), F32),
        compiler_params=_params(1),
        name="out_proj",
    )(x, attn, ssm, ga, gs, wa, ws, gp)


def _ffn_kernel(x_ref, g1_ref, wup_ref, wdn_ref, g2_ref, o_ref, hn_ref, acc_ref):
    f = pl.program_id(1)

    @pl.when(f == 0)
    def _():
        hn_ref[...] = _rms(x_ref[...], g1_ref[...]).astype(BF16)
        acc_ref[...] = jnp.zeros(acc_ref.shape, F32)

    a = jnp.square(jnp.maximum(_dot(hn_ref[...], wup_ref[...]), 0.0)).astype(BF16)
    acc_ref[...] += _dot(a, wdn_ref[...])

    @pl.when(f == pl.num_programs(1) - 1)
    def _():
        o_ref[...] = x_ref[...] + _rms(acc_ref[...], g2_ref[...])


def _ffn(x, g1, wup, wdn, g2, *, tm, tf):
    t, d = x.shape
    d_ff = wup.shape[1]
    row = lambda i, f: (i, 0)
    return pl.pallas_call(
        _ffn_kernel,
        grid=(t // tm, d_ff // tf),
        in_specs=[pl.BlockSpec((tm, d), row), _const_spec((1, d)),
                  pl.BlockSpec((d, tf), lambda i, f: (0, f)), pl.BlockSpec((tf, d), lambda i, f: (f, 0)),
                  _const_spec((1, d))],
        out_specs=pl.BlockSpec((tm, d), row),
        out_shape=jax.ShapeDtypeStruct((t, d), F32),
        scratch_shapes=[pltpu.VMEM((tm, d), BF16), pltpu.VMEM((tm, d), F32)],
        compiler_params=_params(2),
        name="ffn",
    )(x, g1, wup, wdn, g2)


def _rope_tables(pos, rope):
    half = rope // 2
    inv_freq = jnp.exp(-math.log(ROPE_THETA) * 2.0 * jnp.arange(half, dtype=F32) / rope)
    ang = pos.astype(F32)[:, None] * inv_freq[None, :]
    cos, sin = jnp.cos(ang), jnp.sin(ang)
    pad = jnp.zeros((pos.shape[0], LANE - rope), F32)
    return jnp.concatenate([cos, cos, pad], axis=1), jnp.concatenate([-sin, sin, pad], axis=1)


def _swap_halves(w):
    half = w.shape[-1] // 2
    return jnp.concatenate([w[..., half:], w[..., :half]], axis=-1)


def _pad_lanes(w):
    return jnp.pad(w, [(0, 0)] * (w.ndim - 1) + [(0, LANE - w.shape[-1])])


def _tile(n, pref):
    return pref if n % pref == 0 else n


def kernel(x_prompt, x_sample, cache_kv_latent, cache_k_rope, state_ssm_re, state_ssm_im, page_table,
           g_pre_mix, w_in, g_q_lat, w_uq, g_kv_lat, w_uk, w_uv, lam_re, lam_im, log_dt, b_re, b_im,
           c_re, c_im, d_skip, w_glu, g_attn_out, g_ssm_out, w_out, g_post_mix, g_pre_mlp, w_up,
           w_down, g_post_mlp):
    bsz, seq, d = x_prompt.shape
    n_seq, dec_seq, _ = x_sample.shape
    assert dec_seq == 1
    depth = w_in.shape[0]
    q_lora = g_q_lat.shape[1]
    kv_lora, n_heads, nope = w_uk.shape[1:]
    v_dim = w_uv.shape[3]
    rope = cache_k_rope.shape[3]
    ssm_w = g_ssm_out.shape[1]
    attn_w = g_attn_out.shape[1]
    n_grp, n_state = lam_re.shape[1:]
    page = cache_kv_latent.shape[2]
    past_len = page_table.shape[1] * page
    assert nope == LANE and v_dim == LANE and rope * 2 == LANE
    scale = (nope + rope) ** -0.5
    seg_len = seq // N_SEG

    cos_p, sin_p = _rope_tables(jnp.arange(seq, dtype=jnp.int32), rope)
    cos_s, sin_s = _rope_tables(jnp.full((n_seq,), past_len, jnp.int32), rope)

    tm_p = _tile(seq, 512)
    xp = x_prompt.reshape(bsz * seq, d)
    xs = x_sample.reshape(n_seq, d)
    row2 = lambda a: a.reshape(1, -1)

    outs = [[] for _ in range(8)]
    for l in range(depth):
        wi = w_in[l]
        o_kv, o_r, o_u = q_lora, q_lora + kv_lora, q_lora + kv_lora + rope
        w_kr = wi[:, o_r:o_u]
        w_in_l = jnp.concatenate(
            [wi[:, :o_r], wi[:, o_u:], _pad_lanes(w_kr), _pad_lanes(_swap_halves(w_kr))], axis=1).astype(BF16)
        wq = w_uq[l].reshape(q_lora, n_heads, nope + rope)
        wq_r = wq[:, :, nope:]
        w_uq_l = jnp.concatenate(
            [wq[:, :, :nope].reshape(q_lora, -1), _pad_lanes(wq_r).reshape(q_lora, -1),
             _pad_lanes(_swap_halves(wq_r)).reshape(q_lora, -1)], axis=1).astype(BF16)
        w_uk_l = w_uk[l].reshape(kv_lora, n_heads * nope).astype(BF16)
        w_uv_l = w_uv[l].reshape(kv_lora, n_heads * v_dim).astype(BF16)
        w_glu_l = w_glu[l].astype(BF16)
        w_oa_l = w_out[l][:attn_w].astype(BF16)
        w_os_l = w_out[l][attn_w:].astype(BF16)
        w_up_l = w_up[l].astype(BF16)
        w_dn_l = w_down[l].astype(BF16)

        ab_re, ab_im, ap_re, ap_im, bb_re, bb_im = _s5_disc(
            lam_re[l], lam_im[l], log_dt[l], b_re[l], b_im[l], seg_len=seg_len)
        wbr, wbi, cdr, cdi = _s5_dense(bb_re, bb_im, c_re[l], c_im[l])
        d_l = d_skip[l].reshape(-1)

        in_kw = dict(q_lora=q_lora, kv_lora=kv_lora, ssm_w=ssm_w, rope=rope)
        qlat, ckv, ckvb, kr, krb, u = _in_proj(
            xp, row2(g_pre_mix[l]), w_in_l, row2(g_q_lat[l]), row2(g_kv_lat[l]), cos_p, sin_p,
            tm=tm_p, **in_kw)
        q, k, v = _qkv_prompt(qlat, ckvb, krb, w_uq_l, w_uk_l, w_uv_l, cos_p, sin_p,
                              tm=tm_p, n_heads=n_heads, scale=scale)
        attn = _flash(q, k, v, bsz=bsz, seq=seq, n_heads=n_heads, blk=_tile(seq, 512))
        ssm, hf_r, hf_i = _s5_prompt(u, wbr, wbi, ab_re, ab_im, ap_re, ap_im, cdr, cdi, d_l, w_glu_l,
                                     bsz=bsz, seq=seq, sb=_tile(seg_len, 32))
        x1 = _out_proj(xp, attn, ssm, row2(g_attn_out[l]), row2(g_ssm_out[l]), w_oa_l, w_os_l,
                       row2(g_post_mix[l]), tm=tm_p)
        xp = _ffn(x1, row2(g_pre_mlp[l]), w_up_l, w_dn_l, row2(g_post_mlp[l]),
                  tm=tm_p, tf=_tile(w_up_l.shape[1], 1024))
        outs[0].append(ckv.reshape(bsz, seq, kv_lora))
        outs[1].append(kr.reshape(bsz, seq, rope))
        outs[2].append(hf_r.reshape(bsz, n_grp, n_state))
        outs[3].append(hf_i.reshape(bsz, n_grp, n_state))

        qlat, ckv, ckvb, kr, krb, u = _in_proj(
            xs, row2(g_pre_mix[l]), w_in_l, row2(g_q_lat[l]), row2(g_kv_lat[l]), cos_s, sin_s,
            tm=n_seq, **in_kw)
        qs = _q_sample(qlat, w_uq_l, w_uk_l, cos_s, sin_s, n_heads=n_heads, scale=scale)
        o_lat = _paged(page_table, qs.reshape(n_seq, n_heads, kv_lora + LANE),
                       ckv.reshape(n_seq, 1, kv_lora), _pad_lanes(kr).reshape(n_seq, 1, LANE),
                       cache_kv_latent, cache_k_rope, layer=l, n_heads=n_heads)
        attn = _uv_proj(o_lat.reshape(n_seq, n_heads * kv_lora), w_uv_l, n_heads=n_heads)
        ssm, h_r, h_i = _s5_sample(u, state_ssm_re[l], state_ssm_im[l], wbr, wbi, ab_re, ab_im,
                                   cdr, cdi, d_l, w_glu_l)
        x1 = _out_proj(xs, attn, ssm, row2(g_attn_out[l]), row2(g_ssm_out[l]), w_oa_l, w_os_l,
                       row2(g_post_mix[l]), tm=n_seq)
        xs = _ffn(x1, row2(g_pre_mlp[l]), w_up_l, w_dn_l, row2(g_post_mlp[l]),
                  tm=n_seq, tf=_tile(w_up_l.shape[1], 1024))
        outs[4].append(ckv.reshape(n_seq, 1, kv_lora))
        outs[5].append(kr.reshape(n_seq, 1, rope))
        outs[6].append(h_r.reshape(n_seq, n_grp, n_state))
        outs[7].append(h_i.reshape(n_seq, n_grp, n_state))

    return (xp.reshape(bsz, seq, d), xs.reshape(n_seq, 1, d)) + tuple(jnp.stack(o) for o in outs)
```

```python
import functools
import math

import jax
import jax.numpy as jnp
from jax import lax
from jax.experimental import pallas as pl
from jax.experimental.pallas import tpu as pltpu

F32 = jnp.float32
BF16 = jnp.bfloat16

EPS = 1e-6
ROPE_THETA = 10000.0
LANE = 128
SUBLANE = 8
MXU_DIM = 256
VMEM_LIMIT = 56 * 1024 * 1024
N_SEG = SUBLANE


def _params(n_axes):
    return pltpu.CompilerParams(dimension_semantics=("arbitrary",) * n_axes,
                                vmem_limit_bytes=VMEM_LIMIT)


def _const_spec(shape):
    zeros = (0,) * len(shape)
    return pl.BlockSpec(shape, lambda *_: zeros)


def _rms(x, g):
    return x * lax.rsqrt(jnp.mean(x * x, axis=-1, keepdims=True) + EPS) * g


def _dot(a, b):
    return jnp.dot(a, b, preferred_element_type=F32)


def _dot_nt(a, b):
    return lax.dot_general(a, b, (((1,), (1,)), ((), ())), preferred_element_type=F32)


def _in_proj_kernel(x_ref, g_ref, w_ref, gq_ref, gkv_ref, cos_ref, sin_ref,
                    qlat_ref, ckv_ref, ckvb_ref, kr_ref, krb_ref, u_ref,
                    *, q_lora, kv_lora, ssm_w, rope):
    h = _rms(x_ref[...], g_ref[...]).astype(BF16)
    z = _dot(h, w_ref[...])
    o_kv = q_lora
    o_u = o_kv + kv_lora
    o_r = o_u + ssm_w
    qlat_ref[...] = _rms(z[:, :o_kv], gq_ref[...]).astype(BF16)
    ckv = _rms(z[:, o_kv:o_u], gkv_ref[...])
    ckv_ref[...] = ckv
    ckvb_ref[...] = ckv.astype(BF16)
    u_ref[...] = z[:, o_u:o_r]
    rot = z[:, o_r:o_r + LANE] * cos_ref[...] + z[:, o_r + LANE:o_r + 2 * LANE] * sin_ref[...]
    kr_ref[...] = rot[:, :rope]
    krb_ref[...] = rot.astype(BF16)


def _in_proj(x, g, w, gq, gkv, cos, sin, *, tm, q_lora, kv_lora, ssm_w, rope):
    t, d = x.shape
    n = w.shape[1]
    n_pos_blk = cos.shape[0] // tm
    row = lambda i: (i, 0)
    pos = lambda i: (i % n_pos_blk, 0)
    return pl.pallas_call(
        functools.partial(_in_proj_kernel, q_lora=q_lora, kv_lora=kv_lora, ssm_w=ssm_w, rope=rope),
        grid=(t // tm,),
        in_specs=[pl.BlockSpec((tm, d), row), _const_spec((1, d)), _const_spec((d, n)),
                  _const_spec((1, q_lora)), _const_spec((1, kv_lora)),
                  pl.BlockSpec((tm, LANE), pos), pl.BlockSpec((tm, LANE), pos)],
        out_specs=[pl.BlockSpec((tm, q_lora), row), pl.BlockSpec((tm, kv_lora), row),
                   pl.BlockSpec((tm, kv_lora), row), pl.BlockSpec((tm, rope), row),
                   pl.BlockSpec((tm, LANE), row), pl.BlockSpec((tm, ssm_w), row)],
        out_shape=[jax.ShapeDtypeStruct((t, q_lora), BF16), jax.ShapeDtypeStruct((t, kv_lora), F32),
                   jax.ShapeDtypeStruct((t, kv_lora), BF16), jax.ShapeDtypeStruct((t, rope), F32),
                   jax.ShapeDtypeStruct((t, LANE), BF16), jax.ShapeDtypeStruct((t, ssm_w), F32)],
        compiler_params=_params(1),
        name="in_proj",
    )(x, g, w, gq, gkv, cos, sin)


def _qkv_prompt_kernel(qlat_ref, ckvb_ref, krb_ref, wuq_ref, wuk_ref, wuv_ref, cos_ref, sin_ref,
                       q_ref, k_ref, v_ref, *, n_heads, scale):
    z = _dot(qlat_ref[...], wuq_ref[...])
    ckvb = ckvb_ref[...]
    kn = _dot(ckvb, wuk_ref[...])
    v_ref[...] = _dot(ckvb, wuv_ref[...]).astype(BF16)
    cos = cos_ref[...]
    sin = sin_ref[...]
    krb = krb_ref[...]
    o_r = n_heads * LANE
    o_s = 2 * n_heads * LANE
    for h in range(n_heads):
        lo = h * LANE
        hd = h * MXU_DIM
        q_ref[:, hd:hd + LANE] = (z[:, lo:lo + LANE] * scale).astype(BF16)
        rot = z[:, o_r + lo:o_r + lo + LANE] * cos + z[:, o_s + lo:o_s + lo + LANE] * sin
        q_ref[:, hd + LANE:hd + 2 * LANE] = (rot * scale).astype(BF16)
        k_ref[:, hd:hd + LANE] = kn[:, lo:lo + LANE].astype(BF16)
        k_ref[:, hd + LANE:hd + 2 * LANE] = krb


def _qkv_prompt(qlat, ckvb, krb, wuq, wuk, wuv, cos, sin, *, tm, n_heads, scale):
    t, q_lora = qlat.shape
    kv_lora = ckvb.shape[1]
    n_pos_blk = cos.shape[0] // tm
    row = lambda i: (i, 0)
    pos = lambda i: (i % n_pos_blk, 0)
    hw = n_heads * MXU_DIM
    vw = wuv.shape[1]
    return pl.pallas_call(
        functools.partial(_qkv_prompt_kernel, n_heads=n_heads, scale=scale),
        grid=(t // tm,),
        in_specs=[pl.BlockSpec((tm, q_lora), row), pl.BlockSpec((tm, kv_lora), row),
                  pl.BlockSpec((tm, LANE), row), _const_spec(wuq.shape), _const_spec(wuk.shape),
                  _const_spec(wuv.shape), pl.BlockSpec((tm, LANE), pos), pl.BlockSpec((tm, LANE), pos)],
        out_specs=[pl.BlockSpec((tm, hw), row), pl.BlockSpec((tm, hw), row), pl.BlockSpec((tm, vw), row)],
        out_shape=[jax.ShapeDtypeStruct((t, hw), BF16), jax.ShapeDtypeStruct((t, hw), BF16),
                   jax.ShapeDtypeStruct((t, vw), BF16)],
        compiler_params=_params(1),
        name="qkv_prompt",
    )(qlat, ckvb, krb, wuq, wuk, wuv, cos, sin)


def _q_sample_kernel(qlat_ref, wuq_ref, wuk_ref, cos_ref, sin_ref, q_ref, *, n_heads, kv_lora, scale):
    z = _dot(qlat_ref[...], wuq_ref[...])
    cos = cos_ref[...]
    sin = sin_ref[...]
    o_r = n_heads * LANE
    o_s = 2 * n_heads * LANE
    hw = kv_lora + LANE
    for h in range(n_heads):
        lo = h * LANE
        qn = z[:, lo:lo + LANE].astype(BF16)
        q_abs = _dot_nt(qn, wuk_ref[:, lo:lo + LANE])
        q_ref[:, h * hw:h * hw + kv_lora] = (q_abs * scale).astype(BF16)
        rot = z[:, o_r + lo:o_r + lo + LANE] * cos + z[:, o_s + lo:o_s + lo + LANE] * sin
        q_ref[:, h * hw + kv_lora:(h + 1) * hw] = (rot * scale).astype(BF16)


def _q_sample(qlat, wuq, wuk, cos, sin, *, n_heads, scale):
    t = qlat.shape[0]
    kv_lora = wuk.shape[0]
    hw = kv_lora + LANE
    return pl.pallas_call(
        functools.partial(_q_sample_kernel, n_heads=n_heads, kv_lora=kv_lora, scale=scale),
        grid=(1,),
        in_specs=[_const_spec(qlat.shape), _const_spec(wuq.shape), _const_spec(wuk.shape),
                  _const_spec(cos.shape), _const_spec(sin.shape)],
        out_specs=_const_spec((t, n_heads * hw)),
        out_shape=jax.ShapeDtypeStruct((t, n_heads * hw), BF16),
        compiler_params=_params(1),
        name="q_sample",
    )(qlat, wuq, wuk, cos, sin)


HEADS_PER_STEP = 4


def _flash_kernel(q_ref, k_ref, v_ref, o_ref, m_ref, acc_ref, *, blk, hps):
    i = pl.program_id(2)
    m_ref[...] = jnp.full(m_ref.shape, -jnp.inf, F32)
    acc_ref[...] = jnp.zeros(acc_ref.shape, F32)
    ones = jnp.ones((blk, LANE), BF16)

    def scores(h, ks):
        cols = slice(h * MXU_DIM, (h + 1) * MXU_DIM)
        return _dot_nt(q_ref[:, cols], k_ref[pl.ds(ks, blk), cols])

    def update(h, s, ks):
        v1 = jnp.concatenate([v_ref[pl.ds(ks, blk), h * LANE:(h + 1) * LANE], ones], axis=-1)
        m_prev = m_ref[h]
        m_new = jnp.maximum(m_prev, jnp.max(s, axis=-1, keepdims=True))
        p = jnp.exp(s - m_new).astype(BF16)
        acc_ref[h] = jnp.exp(m_prev - m_new) * acc_ref[h] + _dot(p, v1)
        m_ref[h] = m_new

    def below_diagonal(j, carry):
        ks = pl.multiple_of(j * blk, blk)
        for h in range(hps):
            update(h, scores(h, ks), ks)
        return carry

    lax.fori_loop(0, i, below_diagonal, 0)

    ks = pl.multiple_of(i * blk, blk)
    rows = lax.broadcasted_iota(jnp.int32, (blk, blk), 0)
    cols = lax.broadcasted_iota(jnp.int32, (blk, blk), 1)
    visible = cols <= rows
    for h in range(hps):
        update(h, jnp.where(visible, scores(h, ks), -jnp.inf), ks)
    for h in range(hps):
        acc = acc_ref[h]
        o_ref[:, h * LANE:(h + 1) * LANE] = acc[:, :LANE] / acc[:, LANE:]


def _flash(q, k, v, *, bsz, seq, n_heads, blk):
    v_dim = v.shape[1] // n_heads
    assert v_dim == LANE
    hps = HEADS_PER_STEP if n_heads % HEADS_PER_STEP == 0 else 1
    nq = seq // blk
    return pl.pallas_call(
        functools.partial(_flash_kernel, blk=blk, hps=hps),
        grid=(bsz, n_heads // hps, nq),
        in_specs=[pl.BlockSpec((blk, hps * MXU_DIM), lambda b, h, i: (b * nq + i, h)),
                  pl.BlockSpec((seq, hps * MXU_DIM), lambda b, h, i: (b, h)),
                  pl.BlockSpec((seq, hps * v_dim), lambda b, h, i: (b, h))],
        out_specs=pl.BlockSpec((blk, hps * v_dim), lambda b, h, i: (b * nq + i, h)),
        out_shape=jax.ShapeDtypeStruct((bsz * seq, n_heads * v_dim), F32),
        scratch_shapes=[pltpu.VMEM((hps, blk, 1), F32), pltpu.VMEM((hps, blk, 2 * LANE), F32)],
        compiler_params=_params(3),
        name="flash_prompt",
    )(q, k, v)


def _paged_kernel(pt_ref, q_ref, cnew_ref, rnew_ref, cc_hbm, cr_hbm, o_ref,
                  cbuf, rbuf, sem_c, sem_r, *, layer, n_pages, page, kv_lora, rope):
    b = pl.program_id(0)
    slot = b & 1

    def page_copies(seq, sl, p):
        pid = pt_ref[seq, p]
        return (pltpu.make_async_copy(cc_hbm.at[layer, pid],
                                      cbuf.at[sl, pl.ds(p * page, page), :], sem_c.at[sl]),
                pltpu.make_async_copy(cr_hbm.at[layer, pid],
                                      rbuf.at[sl, :, pl.ds(p * page, page)], sem_r.at[sl]))

    def start_sequence(seq, sl):
        for p in range(n_pages):
            for cp in page_copies(seq, sl, p):
                cp.start()

    @pl.when(b == 0)
    def _():
        start_sequence(0, 0)

    @pl.when(b + 1 < pl.num_programs(0))
    def _():
        start_sequence(b + 1, 1 - slot)

    for p in range(n_pages):
        for cp in page_copies(b, slot, p):
            cp.wait()

    cb = cbuf[slot].astype(BF16)
    rb = rbuf[slot].astype(BF16)
    q = q_ref[0]
    s = _dot_nt(q[:, :kv_lora], cb) + _dot(q[:, kv_lora:kv_lora + rope], rb)
    cn = cnew_ref[0]
    qf = q.astype(F32)
    s_new = (jnp.sum(qf[:, :kv_lora] * cn, axis=-1, keepdims=True)
             + jnp.sum(qf[:, kv_lora:] * rnew_ref[0], axis=-1, keepdims=True))
    m = jnp.maximum(jnp.max(s, axis=-1, keepdims=True), s_new)
    p_old = jnp.exp(s - m)
    p_new = jnp.exp(s_new - m)
    denom = jnp.sum(p_old, axis=-1, keepdims=True) + p_new
    o_ref[0] = (_dot(p_old.astype(BF16), cb) + p_new * cn) / denom


def _paged(page_table, q, cnew, rnew, cache_c, cache_rt, *, layer, n_heads):
    n_seq, n_pages = page_table.shape
    _, _, page, kv_lora = cache_c.shape
    rope = cache_rt.shape[2]
    hw = q.shape[2]
    n_keys = n_pages * page
    seq3 = lambda b, pt: (b, 0, 0)
    grid_spec = pltpu.PrefetchScalarGridSpec(
        num_scalar_prefetch=1,
        grid=(n_seq,),
        in_specs=[pl.BlockSpec((1, n_heads, hw), seq3), pl.BlockSpec((1, 1, kv_lora), seq3),
                  pl.BlockSpec((1, 1, LANE), seq3),
                  pl.BlockSpec(memory_space=pl.ANY), pl.BlockSpec(memory_space=pl.ANY)],
        out_specs=pl.BlockSpec((1, n_heads, kv_lora), seq3),
        scratch_shapes=[pltpu.VMEM((2, n_keys, kv_lora), F32), pltpu.VMEM((2, rope, n_keys), F32),
                        pltpu.SemaphoreType.DMA((2,)), pltpu.SemaphoreType.DMA((2,))],
    )
    return pl.pallas_call(
        functools.partial(_paged_kernel, layer=layer, n_pages=n_pages, page=page,
                          kv_lora=kv_lora, rope=rope),
        grid_spec=grid_spec,
        out_shape=jax.ShapeDtypeStruct((n_seq, n_heads, kv_lora), F32),
        compiler_params=_params(1),
        name="paged_sample",
    )(page_table, q, cnew, rnew, cache_c, cache_rt)


def _uv_kernel(ol_ref, wuv_ref, o_ref, *, n_heads, kv_lora):
    for h in range(n_heads):
        o_ref[:, h * LANE:(h + 1) * LANE] = _dot(
            ol_ref[:, h * kv_lora:(h + 1) * kv_lora].astype(BF16), wuv_ref[:, h * LANE:(h + 1) * LANE])


def _uv_proj(o_lat, wuv, *, n_heads):
    t = o_lat.shape[0]
    kv_lora = wuv.shape[0]
    return pl.pallas_call(
        functools.partial(_uv_kernel, n_heads=n_heads, kv_lora=kv_lora),
        grid=(1,),
        in_specs=[_const_spec(o_lat.shape), _const_spec(wuv.shape)],
        out_specs=_const_spec((t, wuv.shape[1])),
        out_shape=jax.ShapeDtypeStruct((t, wuv.shape[1]), F32),
        compiler_params=_params(1),
        name="uv_sample",
    )(o_lat, wuv)


def _s5_disc_kernel(lr_ref, li_ref, ldt_ref, bre_ref, bim_ref,
                    abr_ref, abi_ref, apr_ref, api_ref, bbr_ref, bbi_ref, *, n_sq):
    lr = lr_ref[...]
    li = li_ref[...]
    dt = jnp.exp(ldt_ref[...])
    mag = jnp.exp(lr * dt)
    ab_re = mag * jnp.cos(li * dt)
    ab_im = mag * jnp.sin(li * dt)
    den = lr * lr + li * li
    nr = ab_re - 1.0
    f_re = (nr * lr + ab_im * li) / den
    f_im = (ab_im * lr - nr * li) / den
    abr_ref[...] = ab_re
    abi_ref[...] = ab_im
    pr, pi = ab_re, ab_im
    for _ in range(n_sq):
        pr, pi = pr * pr - pi * pi, 2.0 * pr * pi
    apr_ref[...] = pr
    api_ref[...] = pi
    for h in range(bre_ref.shape[0]):
        b_re = bre_ref[h]
        b_im = bim_ref[h]
        bbr_ref[h] = f_re * b_re - f_im * b_im
        bbi_ref[h] = f_re * b_im + f_im * b_re


def _s5_disc(lam_re, lam_im, log_dt, b_re, b_im, *, seg_len):
    g, p = lam_re.shape
    ch = b_re.shape[2]
    n_sq = int(math.log2(seg_len))
    assert 2 ** n_sq == seg_len
    b_re_t = jnp.transpose(b_re, (2, 0, 1))
    b_im_t = jnp.transpose(b_im, (2, 0, 1))
    gp = jax.ShapeDtypeStruct((g, p), F32)
    hgp = jax.ShapeDtypeStruct((ch, g, p), F32)
    return pl.pallas_call(
        functools.partial(_s5_disc_kernel, n_sq=n_sq),
        grid=(1,),
        in_specs=[_const_spec((g, p)), _const_spec((g, p)), _const_spec((g, 1)),
                  _const_spec((ch, g, p)), _const_spec((ch, g, p))],
        out_specs=[_const_spec((g, p))] * 4 + [_const_spec((ch, g, p))] * 2,
        out_shape=[gp, gp, gp, gp, hgp, hgp],
        compiler_params=_params(1),
        name="s5_disc",
    )(lam_re, lam_im, log_dt.reshape(g, 1), b_re_t, b_im_t)


def _s5_dense(bb_re, bb_im, c_re, c_im):
    ch, g, p = bb_re.shape
    gk = MXU_DIM // ch
    go = LANE // ch

    def in_side(bb):
        m = jnp.transpose(bb, (1, 0, 2)).reshape(g // gk, gk, ch, p)
        eye = jnp.eye(gk, dtype=F32)
        return jnp.einsum('kghp,gj->kghjp', m, eye).reshape(g // gk, gk * ch, gk * p).astype(BF16)

    def out_side(c):
        m = jnp.transpose(c, (0, 2, 1)).reshape(g // go, go, p, ch)
        eye = jnp.eye(go, dtype=F32)
        return jnp.einsum('kgph,gj->kgpjh', m, eye).reshape(g // go, go * p, go * ch).astype(BF16)

    return in_side(bb_re), in_side(bb_im), out_side(c_re), out_side(-c_im)


def _s5_readout(hr, hi, u, cdr_ref, cdi_ref, d_ref, wglu_ref):
    n_ob, kc, _ = cdr_ref.shape
    hrb = hr.astype(BF16)
    hib = hi.astype(BF16)
    y = jnp.concatenate(
        [_dot(hrb[:, ob * kc:(ob + 1) * kc], cdr_ref[ob]) + _dot(hib[:, ob * kc:(ob + 1) * kc], cdi_ref[ob])
         for ob in range(n_ob)], axis=-1)
    y = jax.nn.gelu(y + d_ref[...] * u).astype(BF16)
    z = _dot(y, wglu_ref[...])
    w = z.shape[1] // 2
    return z[:, :w] * jax.nn.sigmoid(z[:, w:])


def _s5_project_in(lhs, wbr_ref, wbi_ref, xr_ref, xi_ref, row0=0):
    n_kt, kc, nc = wbr_ref.shape
    rows = slice(row0, row0 + lhs.shape[0])
    for kt in range(n_kt):
        piece = lhs[:, kt * kc:(kt + 1) * kc]
        xr_ref[rows, kt * nc:(kt + 1) * nc] = _dot(piece, wbr_ref[kt])
        xi_ref[rows, kt * nc:(kt + 1) * nc] = _dot(piece, wbi_ref[kt])


def _s5_prompt_kernel(u_ref, wbr_ref, wbi_ref, ar_ref, ai_ref, apr_ref, api_ref,
                      cdr_ref, cdi_ref, d_ref, wglu_ref,
                      ssm_ref, hfr_ref, hfi_ref,
                      uperm_ref, xr_ref, xi_ref, *, sb):
    ps = pl.program_id(1)
    blk = pl.program_id(2)
    rows = sb * N_SEG
    head = slice(0, N_SEG)
    tail = slice(rows, rows + N_SEG)

    @pl.when((ps == 0) & (blk == 0))
    def _():
        xr_ref[head, :] = jnp.zeros((N_SEG, xr_ref.shape[1]), F32)
        xi_ref[head, :] = jnp.zeros((N_SEG, xi_ref.shape[1]), F32)

    @pl.when((ps == 1) & (blk == 0))
    def _():
        apr = apr_ref[...]
        api = api_ref[...]
        in_r = jnp.zeros(apr.shape, F32)
        in_i = jnp.zeros(apr.shape, F32)
        for seg in range(N_SEG):
            loc_r = xr_ref[seg:seg + 1, :]
            loc_i = xi_ref[seg:seg + 1, :]
            xr_ref[seg:seg + 1, :] = in_r
            xi_ref[seg:seg + 1, :] = in_i
            in_r, in_i = (loc_r + apr * in_r - api * in_i, loc_i + apr * in_i + api * in_r)

    for s in range(sb):
        uperm_ref[s * N_SEG:(s + 1) * N_SEG, :] = u_ref[0, :, s, :]
    _s5_project_in(uperm_ref[...].astype(BF16), wbr_ref, wbi_ref, xr_ref, xi_ref, row0=N_SEG)

    def step(s, carry):
        r0 = pl.multiple_of(s * N_SEG, N_SEG)
        r1 = pl.multiple_of(s * N_SEG + N_SEG, N_SEG)
        hr = xr_ref[pl.ds(r0, N_SEG), :]
        hi = xi_ref[pl.ds(r0, N_SEG), :]
        ar = ar_ref[...]
        ai = ai_ref[...]
        nr = ar * hr - ai * hi + xr_ref[pl.ds(r1, N_SEG), :]
        ni = ar * hi + ai * hr + xi_ref[pl.ds(r1, N_SEG), :]
        xr_ref[pl.ds(r1, N_SEG), :] = nr
        xi_ref[pl.ds(r1, N_SEG), :] = ni
        return carry

    lax.fori_loop(0, sb, step, 0)

    @pl.when(ps == 1)
    def _():
        out = _s5_readout(xr_ref[N_SEG:N_SEG + rows, :], xi_ref[N_SEG:N_SEG + rows, :], uperm_ref[...],
                          cdr_ref, cdi_ref, d_ref, wglu_ref)
        for s in range(sb):
            ssm_ref[0, :, s, :] = out[s * N_SEG:(s + 1) * N_SEG, :]
        hfr_ref[0] = xr_ref[tail, :]
        hfi_ref[0] = xi_ref[tail, :]

    xr_ref[head, :] = xr_ref[tail, :]
    xi_ref[head, :] = xi_ref[tail, :]


def _s5_prompt(u, wbr, wbi, ab_re, ab_im, ap_re, ap_im, cdr, cdi, d_skip, wglu, *, bsz, seq, sb):
    width = u.shape[1]
    gp = ab_re.size
    seg_len = seq // N_SEG
    nblk = seg_len // sb
    rows = sb * N_SEG
    u4 = u.reshape(bsz, N_SEG, seg_len, width)
    a_r = jnp.broadcast_to(ab_re.reshape(1, gp), (N_SEG, gp))
    a_i = jnp.broadcast_to(ab_im.reshape(1, gp), (N_SEG, gp))
    blk4 = lambda b, ps, k: (b, 0, k, 0)
    ssm, hf_r, hf_i = pl.pallas_call(
        functools.partial(_s5_prompt_kernel, sb=sb),
        grid=(bsz, 2, nblk),
        in_specs=[pl.BlockSpec((1, N_SEG, sb, width), blk4),
                  _const_spec(wbr.shape), _const_spec(wbi.shape),
                  _const_spec((N_SEG, gp)), _const_spec((N_SEG, gp)),
                  _const_spec((1, gp)), _const_spec((1, gp)),
                  _const_spec(cdr.shape), _const_spec(cdi.shape),
                  _const_spec((1, width)), _const_spec(wglu.shape)],
        out_specs=[pl.BlockSpec((1, N_SEG, sb, width), lambda b, ps, k: (b, 0, k * ps, 0)),
                   pl.BlockSpec((1, N_SEG, gp), lambda b, ps, k: (b, 0, 0)),
                   pl.BlockSpec((1, N_SEG, gp), lambda b, ps, k: (b, 0, 0))],
        out_shape=[jax.ShapeDtypeStruct((bsz, N_SEG, seg_len, width), F32),
                   jax.ShapeDtypeStruct((bsz, N_SEG, gp), F32),
                   jax.ShapeDtypeStruct((bsz, N_SEG, gp), F32)],
        scratch_shapes=[pltpu.VMEM((rows, width), F32),
                        pltpu.VMEM((rows + N_SEG, gp), F32), pltpu.VMEM((rows + N_SEG, gp), F32)],
        compiler_params=_params(3),
        name="s5_prompt",
    )(u4, wbr, wbi, a_r, a_i, ap_re.reshape(1, gp), ap_im.reshape(1, gp), cdr, cdi,
      d_skip.reshape(1, width), wglu)
    return ssm.reshape(bsz * seq, width), hf_r[:, N_SEG - 1], hf_i[:, N_SEG - 1]


def _s5_sample_kernel(u_ref, h0r_ref, h0i_ref, wbr_ref, wbi_ref, ar_ref, ai_ref,
                      cdr_ref, cdi_ref, d_ref, wglu_ref,
                      ssm_ref, hr_ref, hi_ref, xr_ref, xi_ref):
    u = u_ref[...]
    _s5_project_in(u.astype(BF16), wbr_ref, wbi_ref, xr_ref, xi_ref)
    ar = ar_ref[...]
    ai = ai_ref[...]
    h0r = h0r_ref[...]
    h0i = h0i_ref[...]
    hr = ar * h0r - ai * h0i + xr_ref[...]
    hi = ar * h0i + ai * h0r + xi_ref[...]
    hr_ref[...] = hr
    hi_ref[...] = hi
    ssm_ref[...] = _s5_readout(hr, hi, u, cdr_ref, cdi_ref, d_ref, wglu_ref)


def _s5_sample(u, h0_re, h0_im, wbr, wbi, ab_re, ab_im, cdr, cdi, d_skip, wglu):
    t, width = u.shape
    gp = ab_re.size
    st = jax.ShapeDtypeStruct((t, gp), F32)
    args = (u, h0_re.reshape(t, gp), h0_im.reshape(t, gp), wbr, wbi,
            ab_re.reshape(1, gp), ab_im.reshape(1, gp), cdr, cdi, d_skip.reshape(1, width), wglu)
    return pl.pallas_call(
        _s5_sample_kernel,
        grid=(1,),
        in_specs=[_const_spec(a.shape) for a in args],
        out_specs=[_const_spec((t, width)), _const_spec((t, gp)), _const_spec((t, gp))],
        out_shape=[jax.ShapeDtypeStruct((t, width), F32), st, st],
        scratch_shapes=[pltpu.VMEM((t, gp), F32), pltpu.VMEM((t, gp), F32)],
        compiler_params=_params(1),
        name="s5_sample",
    )(*args)


def _out_proj_kernel(x_ref, attn_ref, ssm_ref, ga_ref, gs_ref, wa_ref, ws_ref, gp_ref, o_ref):
    na = _rms(attn_ref[...], ga_ref[...]).astype(BF16)
    ns = _rms(ssm_ref[...], gs_ref[...]).astype(BF16)
    mix = _dot(na, wa_ref[...]) + _dot(ns, ws_ref[...])
    o_ref[...] = x_ref[...] + _rms(mix, gp_ref[...])


def _out_proj(x, attn, ssm, ga, gs, wa, ws, gp, *, tm):
    t, d = x.shape
    aw = attn.shape[1]
    sw = ssm.shape[1]
    row = lambda i: (i, 0)
    return pl.pallas_call(
        _out_proj_kernel,
        grid=(t // tm,),
        in_specs=[pl.BlockSpec((tm, d), row), pl.BlockSpec((tm, aw), row), pl.BlockSpec((tm, sw), row),
                  _const_spec((1, aw)), _const_spec((1, sw)), _const_spec(wa.shape), _const_spec(ws.shape),
                  _const_spec((1, d))],
        out_specs=pl.BlockSpec((tm, d), row),
        out_shape=jax.ShapeDtypeStruct((t, d), F32),
        compiler_params=_params(1),
        name="out_proj",
    )(x, attn, ssm, ga, gs, wa, ws, gp)


def _ffn_kernel(x_ref, g1_ref, wup_ref, wdn_ref, g2_ref, o_ref, hn_ref, acc_ref):
    f = pl.program_id(1)

    @pl.when(f == 0)
    def _():
        hn_ref[...] = _rms(x_ref[...], g1_ref[...]).astype(BF16)
        acc_ref[...] = jnp.zeros(acc_ref.shape, F32)

    a = jnp.square(jnp.maximum(_dot(hn_ref[...], wup_ref[...]), 0.0)).astype(BF16)
    acc_ref[...] += _dot(a, wdn_ref[...])

    @pl.when(f == pl.num_programs(1) - 1)
    def _():
        o_ref[...] = x_ref[...] + _rms(acc_ref[...], g2_ref[...])


def _ffn(x, g1, wup, wdn, g2, *, tm, tf):
    t, d = x.shape
    d_ff = wup.shape[1]
    row = lambda i, f: (i, 0)
    single = pl.Buffered(1)
    return pl.pallas_call(
        _ffn_kernel,
        grid=(t // tm, d_ff // tf),
        in_specs=[pl.BlockSpec((tm, d), row, pipeline_mode=single), _const_spec((1, d)),
                  pl.BlockSpec((d, tf), lambda i, f: (0, f)), pl.BlockSpec((tf, d), lambda i, f: (f, 0)),
                  _const_spec((1, d))],
        out_specs=pl.BlockSpec((tm, d), row, pipeline_mode=single),
        out_shape=jax.ShapeDtypeStruct((t, d), F32),
        scratch_shapes=[pltpu.VMEM((tm, d), BF16), pltpu.VMEM((tm, d), F32)],
        compiler_params=_params(2),
        name="ffn",
    )(x, g1, wup, wdn, g2)


def _rope_tables(pos, rope):
    half = rope // 2
    inv_freq = jnp.exp(-math.log(ROPE_THETA) * 2.0 * jnp.arange(half, dtype=F32) / rope)
    ang = pos.astype(F32)[:, None] * inv_freq[None, :]
    cos, sin = jnp.cos(ang), jnp.sin(ang)
    pad = jnp.zeros((pos.shape[0], LANE - rope), F32)
    return jnp.concatenate([cos, cos, pad], axis=1), jnp.concatenate([-sin, sin, pad], axis=1)


def _swap_halves(w):
    half = w.shape[-1] // 2
    return jnp.concatenate([w[..., half:], w[..., :half]], axis=-1)


def _pad_lanes(w):
    return jnp.pad(w, [(0, 0)] * (w.ndim - 1) + [(0, LANE - w.shape[-1])])


def _tile(n, pref):
    return pref if n % pref == 0 else n


def kernel(x_prompt, x_sample, cache_kv_latent, cache_k_rope, state_ssm_re, state_ssm_im, page_table,
           g_pre_mix, w_in, g_q_lat, w_uq, g_kv_lat, w_uk, w_uv, lam_re, lam_im, log_dt, b_re, b_im,
           c_re, c_im, d_skip, w_glu, g_attn_out, g_ssm_out, w_out, g_post_mix, g_pre_mlp, w_up,
           w_down, g_post_mlp):
    bsz, seq, d = x_prompt.shape
    n_seq, dec_seq, _ = x_sample.shape
    assert dec_seq == 1
    depth = w_in.shape[0]
    q_lora = g_q_lat.shape[1]
    kv_lora, n_heads, nope = w_uk.shape[1:]
    v_dim = w_uv.shape[3]
    rope = cache_k_rope.shape[3]
    ssm_w = g_ssm_out.shape[1]
    attn_w = g_attn_out.shape[1]
    n_grp, n_state = lam_re.shape[1:]
    page = cache_kv_latent.shape[2]
    past_len = page_table.shape[1] * page
    assert nope == LANE and v_dim == LANE and rope * 2 == LANE
    scale = (nope + rope) ** -0.5
    seg_len = seq // N_SEG

    cos_p, sin_p = _rope_tables(jnp.arange(seq, dtype=jnp.int32), rope)
    cos_s, sin_s = _rope_tables(jnp.full((n_seq,), past_len, jnp.int32), rope)

    tm_p = _tile(seq, 512)
    cache_rt = jnp.swapaxes(cache_k_rope, 2, 3)
    xp = x_prompt.reshape(bsz * seq, d)
    xs = x_sample.reshape(n_seq, d)
    row2 = lambda a: a.reshape(1, -1)

    outs = [[] for _ in range(8)]
    for l in range(depth):
        wi = w_in[l]
        o_kv, o_r, o_u = q_lora, q_lora + kv_lora, q_lora + kv_lora + rope
        w_kr = wi[:, o_r:o_u]
        w_in_l = jnp.concatenate(
            [wi[:, :o_r], wi[:, o_u:], _pad_lanes(w_kr), _pad_lanes(_swap_halves(w_kr))], axis=1).astype(BF16)
        wq = w_uq[l].reshape(q_lora, n_heads, nope + rope)
        wq_r = wq[:, :, nope:]
        w_uq_l = jnp.concatenate(
            [wq[:, :, :nope].reshape(q_lora, -1), _pad_lanes(wq_r).reshape(q_lora, -1),
             _pad_lanes(_swap_halves(wq_r)).reshape(q_lora, -1)], axis=1).astype(BF16)
        w_uk_l = w_uk[l].reshape(kv_lora, n_heads * nope).astype(BF16)
        w_uv_l = w_uv[l].reshape(kv_lora, n_heads * v_dim).astype(BF16)
        w_glu_l = w_glu[l].astype(BF16)
        w_oa_l = w_out[l][:attn_w].astype(BF16)
        w_os_l = w_out[l][attn_w:].astype(BF16)
        w_up_l = w_up[l].astype(BF16)
        w_dn_l = w_down[l].astype(BF16)

        ab_re, ab_im, ap_re, ap_im, bb_re, bb_im = _s5_disc(
            lam_re[l], lam_im[l], log_dt[l], b_re[l], b_im[l], seg_len=seg_len)
        wbr, wbi, cdr, cdi = _s5_dense(bb_re, bb_im, c_re[l], c_im[l])
        d_l = d_skip[l].reshape(-1)

        in_kw = dict(q_lora=q_lora, kv_lora=kv_lora, ssm_w=ssm_w, rope=rope)
        qlat, ckv, ckvb, kr, krb, u = _in_proj(
            xp, row2(g_pre_mix[l]), w_in_l, row2(g_q_lat[l]), row2(g_kv_lat[l]), cos_p, sin_p,
            tm=tm_p, **in_kw)
        q, k, v = _qkv_prompt(qlat, ckvb, krb, w_uq_l, w_uk_l, w_uv_l, cos_p, sin_p,
                              tm=tm_p, n_heads=n_heads, scale=scale)
        attn = _flash(q, k, v, bsz=bsz, seq=seq, n_heads=n_heads, blk=_tile(seq, 512))
        ssm, hf_r, hf_i = _s5_prompt(u, wbr, wbi, ab_re, ab_im, ap_re, ap_im, cdr, cdi, d_l, w_glu_l,
                                     bsz=bsz, seq=seq, sb=_tile(seg_len, 32))
        x1 = _out_proj(xp, attn, ssm, row2(g_attn_out[l]), row2(g_ssm_out[l]), w_oa_l, w_os_l,
                       row2(g_post_mix[l]), tm=tm_p)
        xp = _ffn(x1, row2(g_pre_mlp[l]), w_up_l, w_dn_l, row2(g_post_mlp[l]),
                  tm=_tile(bsz * seq, 1024), tf=_tile(w_up_l.shape[1], 512))
        outs[0].append(ckv.reshape(bsz, seq, kv_lora))
        outs[1].append(kr.reshape(bsz, seq, rope))
        outs[2].append(hf_r.reshape(bsz, n_grp, n_state))
        outs[3].append(hf_i.reshape(bsz, n_grp, n_state))

        qlat, ckv, ckvb, kr, krb, u = _in_proj(
            xs, row2(g_pre_mix[l]), w_in_l, row2(g_q_lat[l]), row2(g_kv_lat[l]), cos_s, sin_s,
            tm=n_seq, **in_kw)
        qs = _q_sample(qlat, w_uq_l, w_uk_l, cos_s, sin_s, n_heads=n_heads, scale=scale)
        o_lat = _paged(page_table, qs.reshape(n_seq, n_heads, kv_lora + LANE),
                       ckv.reshape(n_seq, 1, kv_lora), _pad_lanes(kr).reshape(n_seq, 1, LANE),
                       cache_kv_latent, cache_rt, layer=l, n_heads=n_heads)
        attn = _uv_proj(o_lat.reshape(n_seq, n_heads * kv_lora), w_uv_l, n_heads=n_heads)
        ssm, h_r, h_i = _s5_sample(u, state_ssm_re[l], state_ssm_im[l], wbr, wbi, ab_re, ab_im,
                                   cdr, cdi, d_l, w_glu_l)
        x1 = _out_proj(xs, attn, ssm, row2(g_attn_out[l]), row2(g_ssm_out[l]), w_oa_l, w_os_l,
                       row2(g_post_mix[l]), tm=n_seq)
        xs = _ffn(x1, row2(g_pre_mlp[l]), w_up_l, w_dn_l, row2(g_post_mlp[l]),
                  tm=n_seq, tf=_tile(w_up_l.shape[1], 1024))
        outs[4].append(ckv.reshape(n_seq, 1, kv_lora))
        outs[5].append(kr.reshape(n_seq, 1, rope))
        outs[6].append(h_r.reshape(n_seq, n_grp, n_state))
        outs[7].append(h_i.reshape(n_seq, n_grp, n_state))

    return (xp.reshape(bsz, seq, d), xs.reshape(n_seq, 1, d)) + tuple(jnp.stack(o) for o in outs)
```

```python
import functools
import math

import jax
import jax.numpy as jnp
from jax import lax
from jax.experimental import pallas as pl
from jax.experimental.pallas import tpu as pltpu

F32 = jnp.float32
BF16 = jnp.bfloat16

EPS = 1e-6
ROPE_THETA = 10000.0
LANE = 128
SUBLANE = 8
MXU_DIM = 256
VMEM_LIMIT = 56 * 1024 * 1024
N_SEG = SUBLANE
SCAN_LANES = 8 * LANE


def _params(n_axes):
    return pltpu.CompilerParams(dimension_semantics=("arbitrary",) * n_axes,
                                vmem_limit_bytes=VMEM_LIMIT)


def _const_spec(shape):
    zeros = (0,) * len(shape)
    return pl.BlockSpec(shape, lambda *_: zeros)


def _layer_spec(arr, layer):
    tail = tuple(arr.shape[1:])
    zeros = (0,) * len(tail)
    return pl.BlockSpec((None,) + tail, lambda *_: (layer,) + zeros)


def _rms(x, g):
    return x * lax.rsqrt(jnp.mean(x * x, axis=-1, keepdims=True) + EPS) * g


def _dot(a, b):
    return jnp.dot(a, b, preferred_element_type=F32)


def _dot_nt(a, b):
    return lax.dot_general(a, b, (((1,), (1,)), ((), ())), preferred_element_type=F32)


def _in_proj_kernel(x_ref, g_ref, w_ref, gq_ref, gkv_ref, cos_ref, sin_ref,
                    qlat_ref, ckv_ref, ckvb_ref, kr_ref, krb_ref, u_ref,
                    *, q_lora, kv_lora, ssm_w, rope):
    h = _rms(x_ref[...], g_ref[...]).astype(BF16)
    z = _dot(h, w_ref[...])
    o_kv = q_lora
    o_u = o_kv + kv_lora
    o_r = o_u + ssm_w
    qlat_ref[...] = _rms(z[:, :o_kv], gq_ref[...]).astype(BF16)
    ckv = _rms(z[:, o_kv:o_u], gkv_ref[...])
    ckv_ref[...] = ckv
    ckvb_ref[...] = ckv.astype(BF16)
    u_ref[...] = z[:, o_u:o_r]
    rot = z[:, o_r:o_r + LANE] * cos_ref[...] + z[:, o_r + LANE:o_r + 2 * LANE] * sin_ref[...]
    kr_ref[...] = rot[:, :rope]
    krb_ref[...] = rot.astype(BF16)


def _in_proj(x, g, w, gq, gkv, cos, sin, *, layer, tm, q_lora, kv_lora, ssm_w, rope):
    t, d = x.shape
    n_pos_blk = cos.shape[0] // tm
    row = lambda i: (i, 0)
    pos = lambda i: (i % n_pos_blk, 0)
    return pl.pallas_call(
        functools.partial(_in_proj_kernel, q_lora=q_lora, kv_lora=kv_lora, ssm_w=ssm_w, rope=rope),
        grid=(t // tm,),
        in_specs=[pl.BlockSpec((tm, d), row), _layer_spec(g, layer), _layer_spec(w, layer),
                  _layer_spec(gq, layer), _layer_spec(gkv, layer),
                  pl.BlockSpec((tm, LANE), pos), pl.BlockSpec((tm, LANE), pos)],
        out_specs=[pl.BlockSpec((tm, q_lora), row), pl.BlockSpec((tm, kv_lora), row),
                   pl.BlockSpec((tm, kv_lora), row), pl.BlockSpec((tm, rope), row),
                   pl.BlockSpec((tm, LANE), row), pl.BlockSpec((tm, ssm_w), row)],
        out_shape=[jax.ShapeDtypeStruct((t, q_lora), BF16), jax.ShapeDtypeStruct((t, kv_lora), F32),
                   jax.ShapeDtypeStruct((t, kv_lora), BF16), jax.ShapeDtypeStruct((t, rope), F32),
                   jax.ShapeDtypeStruct((t, LANE), BF16), jax.ShapeDtypeStruct((t, ssm_w), F32)],
        compiler_params=_params(1),
        name="in_proj",
    )(x, g, w, gq, gkv, cos, sin)


def _qkv_prompt_kernel(qlat_ref, ckvb_ref, krb_ref, wuq_ref, wuk_ref, wuv_ref, cos_ref, sin_ref,
                       q_ref, k_ref, v_ref, *, n_heads, scale):
    z = _dot(qlat_ref[...], wuq_ref[...])
    ckvb = ckvb_ref[...]
    kn = _dot(ckvb, wuk_ref[...])
    v_ref[...] = _dot(ckvb, wuv_ref[...]).astype(BF16)
    cos = cos_ref[...]
    sin = sin_ref[...]
    krb = krb_ref[...]
    o_r = n_heads * LANE
    o_s = 2 * n_heads * LANE
    for h in range(n_heads):
        lo = h * LANE
        hd = h * MXU_DIM
        q_ref[:, hd:hd + LANE] = (z[:, lo:lo + LANE] * scale).astype(BF16)
        rot = z[:, o_r + lo:o_r + lo + LANE] * cos + z[:, o_s + lo:o_s + lo + LANE] * sin
        q_ref[:, hd + LANE:hd + 2 * LANE] = (rot * scale).astype(BF16)
        k_ref[:, hd:hd + LANE] = kn[:, lo:lo + LANE].astype(BF16)
        k_ref[:, hd + LANE:hd + 2 * LANE] = krb


def _qkv_prompt(qlat, ckvb, krb, wuq, wuk, wuv, cos, sin, *, layer, tm, n_heads, scale):
    t, q_lora = qlat.shape
    kv_lora = ckvb.shape[1]
    n_pos_blk = cos.shape[0] // tm
    row = lambda i: (i, 0)
    pos = lambda i: (i % n_pos_blk, 0)
    hw = n_heads * MXU_DIM
    vw = wuv.shape[2]
    return pl.pallas_call(
        functools.partial(_qkv_prompt_kernel, n_heads=n_heads, scale=scale),
        grid=(t // tm,),
        in_specs=[pl.BlockSpec((tm, q_lora), row), pl.BlockSpec((tm, kv_lora), row),
                  pl.BlockSpec((tm, LANE), row), _layer_spec(wuq, layer), _layer_spec(wuk, layer),
                  _layer_spec(wuv, layer), pl.BlockSpec((tm, LANE), pos), pl.BlockSpec((tm, LANE), pos)],
        out_specs=[pl.BlockSpec((tm, hw), row), pl.BlockSpec((tm, hw), row), pl.BlockSpec((tm, vw), row)],
        out_shape=[jax.ShapeDtypeStruct((t, hw), BF16), jax.ShapeDtypeStruct((t, hw), BF16),
                   jax.ShapeDtypeStruct((t, vw), BF16)],
        compiler_params=_params(1),
        name="qkv_prompt",
    )(qlat, ckvb, krb, wuq, wuk, wuv, cos, sin)


def _q_sample_kernel(qlat_ref, wuq_ref, wuk_ref, cos_ref, sin_ref, q_ref, *, n_heads, kv_lora, scale):
    z = _dot(qlat_ref[...], wuq_ref[...])
    cos = cos_ref[...]
    sin = sin_ref[...]
    o_r = n_heads * LANE
    o_s = 2 * n_heads * LANE
    hw = kv_lora + LANE
    for h in range(n_heads):
        lo = h * LANE
        qn = z[:, lo:lo + LANE].astype(BF16)
        q_abs = _dot_nt(qn, wuk_ref[:, lo:lo + LANE])
        q_ref[:, h * hw:h * hw + kv_lora] = (q_abs * scale).astype(BF16)
        rot = z[:, o_r + lo:o_r + lo + LANE] * cos + z[:, o_s + lo:o_s + lo + LANE] * sin
        q_ref[:, h * hw + kv_lora:(h + 1) * hw] = (rot * scale).astype(BF16)


def _q_sample(qlat, wuq, wuk, cos, sin, *, layer, n_heads, scale):
    t = qlat.shape[0]
    kv_lora = wuk.shape[1]
    hw = kv_lora + LANE
    return pl.pallas_call(
        functools.partial(_q_sample_kernel, n_heads=n_heads, kv_lora=kv_lora, scale=scale),
        grid=(1,),
        in_specs=[_const_spec(qlat.shape), _layer_spec(wuq, layer), _layer_spec(wuk, layer),
                  _const_spec(cos.shape), _const_spec(sin.shape)],
        out_specs=_const_spec((t, n_heads * hw)),
        out_shape=jax.ShapeDtypeStruct((t, n_heads * hw), BF16),
        compiler_params=_params(1),
        name="q_sample",
    )(qlat, wuq, wuk, cos, sin)


HEADS_PER_STEP = 4


def _flash_kernel(q_ref, k_ref, v_ref, o_ref, m_ref, acc_ref, *, blk, hps):
    i = pl.program_id(2)
    m_ref[...] = jnp.full(m_ref.shape, -jnp.inf, F32)
    acc_ref[...] = jnp.zeros(acc_ref.shape, F32)
    ones = jnp.ones((blk, LANE), BF16)

    def scores(h, ks):
        cols = slice(h * MXU_DIM, (h + 1) * MXU_DIM)
        return _dot_nt(q_ref[:, cols], k_ref[pl.ds(ks, blk), cols])

    def update(h, s, ks):
        v1 = jnp.concatenate([v_ref[pl.ds(ks, blk), h * LANE:(h + 1) * LANE], ones], axis=-1)
        m_prev = m_ref[h]
        m_new = jnp.maximum(m_prev, jnp.max(s, axis=-1, keepdims=True))
        p = jnp.exp(s - m_new).astype(BF16)
        acc_ref[h] = jnp.exp(m_prev - m_new) * acc_ref[h] + _dot(p, v1)
        m_ref[h] = m_new

    def below_diagonal(j, carry):
        ks = pl.multiple_of(j * blk, blk)
        for h in range(hps):
            update(h, scores(h, ks), ks)
        return carry

    lax.fori_loop(0, i, below_diagonal, 0)

    ks = pl.multiple_of(i * blk, blk)
    rows = lax.broadcasted_iota(jnp.int32, (blk, blk), 0)
    cols = lax.broadcasted_iota(jnp.int32, (blk, blk), 1)
    visible = cols <= rows
    for h in range(hps):
        update(h, jnp.where(visible, scores(h, ks), -jnp.inf), ks)
    for h in range(hps):
        acc = acc_ref[h]
        o_ref[:, h * LANE:(h + 1) * LANE] = acc[:, :LANE] / acc[:, LANE:]


def _flash(q, k, v, *, bsz, seq, n_heads, blk):
    v_dim = v.shape[1] // n_heads
    assert v_dim == LANE
    hps = HEADS_PER_STEP if n_heads % HEADS_PER_STEP == 0 else 1
    nq = seq // blk
    return pl.pallas_call(
        functools.partial(_flash_kernel, blk=blk, hps=hps),
        grid=(bsz, n_heads // hps, nq),
        in_specs=[pl.BlockSpec((blk, hps * MXU_DIM), lambda b, h, i: (b * nq + i, h)),
                  pl.BlockSpec((seq, hps * MXU_DIM), lambda b, h, i: (b, h)),
                  pl.BlockSpec((seq, hps * v_dim), lambda b, h, i: (b, h))],
        out_specs=pl.BlockSpec((blk, hps * v_dim), lambda b, h, i: (b * nq + i, h)),
        out_shape=jax.ShapeDtypeStruct((bsz * seq, n_heads * v_dim), F32),
        scratch_shapes=[pltpu.VMEM((hps, blk, 1), F32), pltpu.VMEM((hps, blk, 2 * LANE), F32)],
        compiler_params=_params(3),
        name="flash_prompt",
    )(q, k, v)


def _paged_kernel(pt_ref, q_ref, cnew_ref, rnew_ref, cc_hbm, cr_hbm, o_ref,
                  cbuf, rbuf, sem_c, sem_r, *, layer, n_pages, page, kv_lora, rope):
    b = pl.program_id(0)
    slot = b & 1

    def page_copies(seq, sl, p):
        pid = pt_ref[seq, p]
        return (pltpu.make_async_copy(cc_hbm.at[layer, pid],
                                      cbuf.at[sl, pl.ds(p * page, page), :], sem_c.at[sl]),
                pltpu.make_async_copy(cr_hbm.at[layer, pid],
                                      rbuf.at[sl, :, pl.ds(p * page, page)], sem_r.at[sl]))

    def start_sequence(seq, sl):
        for p in range(n_pages):
            for cp in page_copies(seq, sl, p):
                cp.start()

    @pl.when(b == 0)
    def _():
        start_sequence(0, 0)

    @pl.when(b + 1 < pl.num_programs(0))
    def _():
        start_sequence(b + 1, 1 - slot)

    for p in range(n_pages):
        for cp in page_copies(b, slot, p):
            cp.wait()

    cb = cbuf[slot].astype(BF16)
    rb = rbuf[slot].astype(BF16)
    q = q_ref[0]
    s = _dot_nt(q[:, :kv_lora], cb) + _dot(q[:, kv_lora:kv_lora + rope], rb)
    cn = cnew_ref[0]
    qf = q.astype(F32)
    s_new = (jnp.sum(qf[:, :kv_lora] * cn, axis=-1, keepdims=True)
             + jnp.sum(qf[:, kv_lora:] * rnew_ref[0], axis=-1, keepdims=True))
    m = jnp.maximum(jnp.max(s, axis=-1, keepdims=True), s_new)
    p_old = jnp.exp(s - m)
    p_new = jnp.exp(s_new - m)
    denom = jnp.sum(p_old, axis=-1, keepdims=True) + p_new
    pb = p_old.astype(BF16)
    half = pb.shape[1] // 2
    pv = _dot(pb[:, :half], cb[:half]) + _dot(pb[:, half:], cb[half:])
    o_ref[0] = (pv + p_new * cn) / denom


def _paged(page_table, q, cnew, rnew, cache_c, cache_rt, *, layer, n_heads):
    n_seq, n_pages = page_table.shape
    _, _, page, kv_lora = cache_c.shape
    rope = cache_rt.shape[2]
    hw = q.shape[2]
    n_keys = n_pages * page
    seq3 = lambda b, pt: (b, 0, 0)
    grid_spec = pltpu.PrefetchScalarGridSpec(
        num_scalar_prefetch=1,
        grid=(n_seq,),
        in_specs=[pl.BlockSpec((1, n_heads, hw), seq3), pl.BlockSpec((1, 1, kv_lora), seq3),
                  pl.BlockSpec((1, 1, LANE), seq3),
                  pl.BlockSpec(memory_space=pl.ANY), pl.BlockSpec(memory_space=pl.ANY)],
        out_specs=pl.BlockSpec((1, n_heads, kv_lora), seq3),
        scratch_shapes=[pltpu.VMEM((2, n_keys, kv_lora), F32), pltpu.VMEM((2, rope, n_keys), F32),
                        pltpu.SemaphoreType.DMA((2,)), pltpu.SemaphoreType.DMA((2,))],
    )
    return pl.pallas_call(
        functools.partial(_paged_kernel, layer=layer, n_pages=n_pages, page=page,
                          kv_lora=kv_lora, rope=rope),
        grid_spec=grid_spec,
        out_shape=jax.ShapeDtypeStruct((n_seq, n_heads, kv_lora), F32),
        compiler_params=_params(1),
        name="paged_sample",
    )(page_table, q, cnew, rnew, cache_c, cache_rt)


def _uv_kernel(ol_ref, wuv_ref, o_ref, *, n_heads, kv_lora):
    for h in range(n_heads):
        o_ref[:, h * LANE:(h + 1) * LANE] = _dot(
            ol_ref[:, h * kv_lora:(h + 1) * kv_lora].astype(BF16), wuv_ref[:, h * LANE:(h + 1) * LANE])


def _uv_proj(o_lat, wuv, *, layer, n_heads):
    t = o_lat.shape[0]
    kv_lora, vw = wuv.shape[1:]
    return pl.pallas_call(
        functools.partial(_uv_kernel, n_heads=n_heads, kv_lora=kv_lora),
        grid=(1,),
        in_specs=[_const_spec(o_lat.shape), _layer_spec(wuv, layer)],
        out_specs=_const_spec((t, vw)),
        out_shape=jax.ShapeDtypeStruct((t, vw), F32),
        compiler_params=_params(1),
        name="uv_sample",
    )(o_lat, wuv)


def _s5_disc_kernel(lr_ref, li_ref, ldt_ref, bre_ref, bim_ref,
                    abr_ref, abi_ref, apr_ref, api_ref, bbr_ref, bbi_ref, *, n_sq):
    lr = lr_ref[...]
    li = li_ref[...]
    dt = jnp.exp(ldt_ref[...])
    mag = jnp.exp(lr * dt)
    ab_re = mag * jnp.cos(li * dt)
    ab_im = mag * jnp.sin(li * dt)
    den = lr * lr + li * li
    nr = ab_re - 1.0
    f_re = (nr * lr + ab_im * li) / den
    f_im = (ab_im * lr - nr * li) / den
    abr_ref[...] = ab_re
    abi_ref[...] = ab_im
    pr, pi = ab_re, ab_im
    for _ in range(n_sq):
        pr, pi = pr * pr - pi * pi, 2.0 * pr * pi
    apr_ref[...] = pr
    api_ref[...] = pi
    for h in range(bre_ref.shape[0]):
        b_re = bre_ref[h]
        b_im = bim_ref[h]
        bbr_ref[h] = f_re * b_re - f_im * b_im
        bbi_ref[h] = f_re * b_im + f_im * b_re


def _s5_disc(lam_re, lam_im, log_dt, b_re, b_im, *, seg_len):
    depth, g, p = lam_re.shape
    ch = b_re.shape[3]
    n_sq = int(math.log2(seg_len))
    assert 2 ** n_sq == seg_len
    b_re_t = jnp.transpose(b_re, (0, 3, 1, 2))
    b_im_t = jnp.transpose(b_im, (0, 3, 1, 2))
    gp_spec = pl.BlockSpec((None, g, p), lambda l: (l, 0, 0))
    hgp_spec = pl.BlockSpec((None, ch, g, p), lambda l: (l, 0, 0, 0))
    gp = jax.ShapeDtypeStruct((depth, g, p), F32)
    hgp = jax.ShapeDtypeStruct((depth, ch, g, p), F32)
    return pl.pallas_call(
        functools.partial(_s5_disc_kernel, n_sq=n_sq),
        grid=(depth,),
        in_specs=[gp_spec, gp_spec, pl.BlockSpec((None, g, 1), lambda l: (l, 0, 0)), hgp_spec, hgp_spec],
        out_specs=[gp_spec] * 4 + [hgp_spec] * 2,
        out_shape=[gp, gp, gp, gp, hgp, hgp],
        compiler_params=_params(1),
        name="s5_disc",
    )(lam_re, lam_im, log_dt.reshape(depth, g, 1), b_re_t, b_im_t)


def _s5_dense(bb_re, bb_im, c_re, c_im):
    depth, ch, g, p = bb_re.shape
    gk = MXU_DIM // ch
    go = LANE // ch

    def in_side(bb):
        m = jnp.transpose(bb, (0, 2, 1, 3)).reshape(depth, g // gk, gk, ch, p)
        eye = jnp.eye(gk, dtype=F32)
        return jnp.einsum('lkghp,gj->lkghjp', m, eye).reshape(depth, g // gk, gk * ch, gk * p).astype(BF16)

    def out_side(c):
        m = jnp.transpose(c, (0, 1, 3, 2)).reshape(depth, g // go, go, p, ch)
        eye = jnp.eye(go, dtype=F32)
        return jnp.einsum('lkgph,gj->lkgpjh', m, eye).reshape(depth, g // go, go * p, go * ch).astype(BF16)

    return in_side(bb_re), in_side(bb_im), out_side(c_re), out_side(-c_im)


def _s5_readout(hr, hi, u, cdr_ref, cdi_ref, d_ref, wglu_ref):
    n_ob, kc, _ = cdr_ref.shape
    hrb = hr.astype(BF16)
    hib = hi.astype(BF16)
    y = jnp.concatenate(
        [_dot(hrb[:, ob * kc:(ob + 1) * kc], cdr_ref[ob]) + _dot(hib[:, ob * kc:(ob + 1) * kc], cdi_ref[ob])
         for ob in range(n_ob)], axis=-1)
    y = jax.nn.gelu(y + d_ref[...] * u).astype(BF16)
    z = _dot(y, wglu_ref[...])
    w = z.shape[1] // 2
    return z[:, :w] * jax.nn.sigmoid(z[:, w:])


def _s5_project_in(lhs, wbr_ref, wbi_ref, xr_ref, xi_ref, row0=0):
    n_kt, kc, nc = wbr_ref.shape
    rows = slice(row0, row0 + lhs.shape[0])
    for kt in range(n_kt):
        piece = lhs[:, kt * kc:(kt + 1) * kc]
        xr_ref[rows, kt * nc:(kt + 1) * nc] = _dot(piece, wbr_ref[kt])
        xi_ref[rows, kt * nc:(kt + 1) * nc] = _dot(piece, wbi_ref[kt])


def _s5_prompt_kernel(u_ref, wbr_ref, wbi_ref, ar_ref, ai_ref, apr_ref, api_ref,
                      cdr_ref, cdi_ref, d_ref, wglu_ref,
                      ssm_ref, hfr_ref, hfi_ref,
                      uperm_ref, xr_ref, xi_ref, *, sb):
    ps = pl.program_id(1)
    blk = pl.program_id(2)
    rows = sb * N_SEG
    gp = xr_ref.shape[1]
    head = slice(0, N_SEG)
    tail = slice(rows, rows + N_SEG)

    @pl.when((ps == 0) & (blk == 0))
    def _():
        xr_ref[head, :] = jnp.zeros((N_SEG, gp), F32)
        xi_ref[head, :] = jnp.zeros((N_SEG, gp), F32)

    @pl.when((ps == 1) & (blk == 0))
    def _():
        apr = apr_ref[...]
        api = api_ref[...]
        in_r = jnp.zeros(apr.shape, F32)
        in_i = jnp.zeros(apr.shape, F32)
        for seg in range(N_SEG):
            loc_r = xr_ref[seg:seg + 1, :]
            loc_i = xi_ref[seg:seg + 1, :]
            xr_ref[seg:seg + 1, :] = in_r
            xi_ref[seg:seg + 1, :] = in_i
            in_r, in_i = (loc_r + apr * in_r - api * in_i, loc_i + apr * in_i + api * in_r)

    for s in range(sb):
        uperm_ref[s * N_SEG:(s + 1) * N_SEG, :] = u_ref[0, :, s, :]
    _s5_project_in(uperm_ref[...].astype(BF16), wbr_ref, wbi_ref, xr_ref, xi_ref, row0=N_SEG)

    strip = min(SCAN_LANES, gp)
    for c in range(gp // strip):
        cols = slice(c * strip, (c + 1) * strip)
        ar = ar_ref[:, cols]
        ai = ai_ref[:, cols]

        def step(s, carry, cols=cols, ar=ar, ai=ai):
            hr, hi = carry
            r1 = pl.multiple_of(s * N_SEG + N_SEG, N_SEG)
            nr = ar * hr - ai * hi + xr_ref[pl.ds(r1, N_SEG), cols]
            ni = ar * hi + ai * hr + xi_ref[pl.ds(r1, N_SEG), cols]
            xr_ref[pl.ds(r1, N_SEG), cols] = nr
            xi_ref[pl.ds(r1, N_SEG), cols] = ni
            return nr, ni

        lax.fori_loop(0, sb, step, (xr_ref[head, cols], xi_ref[head, cols]))

    @pl.when(ps == 1)
    def _():
        out = _s5_readout(xr_ref[N_SEG:N_SEG + rows, :], xi_ref[N_SEG:N_SEG + rows, :], uperm_ref[...],
                          cdr_ref, cdi_ref, d_ref, wglu_ref)
        for s in range(sb):
            ssm_ref[0, :, s, :] = out[s * N_SEG:(s + 1) * N_SEG, :]
        hfr_ref[0] = xr_ref[tail, :]
        hfi_ref[0] = xi_ref[tail, :]

    xr_ref[head, :] = xr_ref[tail, :]
    xi_ref[head, :] = xi_ref[tail, :]


def _s5_prompt(u, wbr, wbi, a_re, a_im, ap_re, ap_im, cdr, cdi, d_skip, wglu, *, layer, bsz, seq, sb):
    width = u.shape[1]
    gp = a_re.shape[2]
    seg_len = seq // N_SEG
    nblk = seg_len // sb
    rows = sb * N_SEG
    u4 = u.reshape(bsz, N_SEG, seg_len, width)
    blk4 = lambda b, ps, k: (b, 0, k, 0)
    weights = (wbr, wbi, a_re, a_im, ap_re, ap_im, cdr, cdi, d_skip, wglu)
    ssm, hf_r, hf_i = pl.pallas_call(
        functools.partial(_s5_prompt_kernel, sb=sb),
        grid=(bsz, 2, nblk),
        in_specs=[pl.BlockSpec((1, N_SEG, sb, width), blk4)] + [_layer_spec(w, layer) for w in weights],
        out_specs=[pl.BlockSpec((1, N_SEG, sb, width), lambda b, ps, k: (b, 0, k * ps, 0)),
                   pl.BlockSpec((1, N_SEG, gp), lambda b, ps, k: (b, 0, 0)),
                   pl.BlockSpec((1, N_SEG, gp), lambda b, ps, k: (b, 0, 0))],
        out_shape=[jax.ShapeDtypeStruct((bsz, N_SEG, seg_len, width), F32),
                   jax.ShapeDtypeStruct((bsz, N_SEG, gp), F32),
                   jax.ShapeDtypeStruct((bsz, N_SEG, gp), F32)],
        scratch_shapes=[pltpu.VMEM((rows, width), F32),
                        pltpu.VMEM((rows + N_SEG, gp), F32), pltpu.VMEM((rows + N_SEG, gp), F32)],
        compiler_params=_params(3),
        name="s5_prompt",
    )(u4, *weights)
    return ssm.reshape(bsz * seq, width), hf_r[:, N_SEG - 1], hf_i[:, N_SEG - 1]


def _s5_sample_kernel(u_ref, h0r_ref, h0i_ref, wbr_ref, wbi_ref, ar_ref, ai_ref,
                      cdr_ref, cdi_ref, d_ref, wglu_ref,
                      ssm_ref, hr_ref, hi_ref, xr_ref, xi_ref):
    u = u_ref[...]
    _s5_project_in(u.astype(BF16), wbr_ref, wbi_ref, xr_ref, xi_ref)
    ar = ar_ref[0:1, :]
    ai = ai_ref[0:1, :]
    h0r = h0r_ref[...]
    h0i = h0i_ref[...]
    hr = ar * h0r - ai * h0i + xr_ref[...]
    hi = ar * h0i + ai * h0r + xi_ref[...]
    hr_ref[...] = hr
    hi_ref[...] = hi
    ssm_ref[...] = _s5_readout(hr, hi, u, cdr_ref, cdi_ref, d_ref, wglu_ref)


def _s5_sample(u, h0_re, h0_im, wbr, wbi, a_re, a_im, cdr, cdi, d_skip, wglu, *, layer):
    t, width = u.shape
    gp = a_re.shape[2]
    st = jax.ShapeDtypeStruct((t, gp), F32)
    weights = (wbr, wbi, a_re, a_im, cdr, cdi, d_skip, wglu)
    return pl.pallas_call(
        _s5_sample_kernel,
        grid=(1,),
        in_specs=([_const_spec((t, width)), _layer_spec(h0_re, layer), _layer_spec(h0_im, layer)]
                  + [_layer_spec(w, layer) for w in weights]),
        out_specs=[_const_spec((t, width)), _const_spec((t, gp)), _const_spec((t, gp))],
        out_shape=[jax.ShapeDtypeStruct((t, width), F32), st, st],
        scratch_shapes=[pltpu.VMEM((t, gp), F32), pltpu.VMEM((t, gp), F32)],
        compiler_params=_params(1),
        name="s5_sample",
    )(u, h0_re, h0_im, *weights)


def _out_proj_kernel(x_ref, attn_ref, ssm_ref, ga_ref, gs_ref, wa_ref, ws_ref, gp_ref, o_ref):
    na = _rms(attn_ref[...], ga_ref[...]).astype(BF16)
    ns = _rms(ssm_ref[...], gs_ref[...]).astype(BF16)
    mix = _dot(na, wa_ref[...]) + _dot(ns, ws_ref[...])
    o_ref[...] = x_ref[...] + _rms(mix, gp_ref[...])


def _out_proj(x, attn, ssm, ga, gs, w_out, gp, *, layer, tm):
    t, d = x.shape
    aw = attn.shape[1]
    sw = ssm.shape[1]
    assert aw == sw and w_out.shape[1] == aw + sw
    row = lambda i: (i, 0)
    return pl.pallas_call(
        _out_proj_kernel,
        grid=(t // tm,),
        in_specs=[pl.BlockSpec((tm, d), row), pl.BlockSpec((tm, aw), row), pl.BlockSpec((tm, sw), row),
                  _layer_spec(ga, layer), _layer_spec(gs, layer),
                  pl.BlockSpec((None, aw, d), lambda i: (layer, 0, 0)),
                  pl.BlockSpec((None, sw, d), lambda i: (layer, 1, 0)),
                  _layer_spec(gp, layer)],
        out_specs=pl.BlockSpec((tm, d), row),
        out_shape=jax.ShapeDtypeStruct((t, d), F32),
        compiler_params=_params(1),
        name="out_proj",
    )(x, attn, ssm, ga, gs, w_out, w_out, gp)


def _ffn_kernel(x_ref, g1_ref, wup_ref, wdn_ref, g2_ref, o_ref, hn_ref, acc_ref):
    f = pl.program_id(1)

    @pl.when(f == 0)
    def _():
        hn_ref[...] = _rms(x_ref[...], g1_ref[...]).astype(BF16)
        acc_ref[...] = jnp.zeros(acc_ref.shape, F32)

    a = jnp.square(jnp.maximum(_dot(hn_ref[...], wup_ref[...]), 0.0)).astype(BF16)
    acc_ref[...] += _dot(a, wdn_ref[...])

    @pl.when(f == pl.num_programs(1) - 1)
    def _():
        o_ref[...] = x_ref[...] + _rms(acc_ref[...], g2_ref[...])


def _ffn(x, g1, wup, wdn, g2, *, layer, tm, tf):
    t, d = x.shape
    d_ff = wup.shape[2]
    row = lambda i, f: (i, 0)
    return pl.pallas_call(
        _ffn_kernel,
        grid=(t // tm, d_ff // tf),
        in_specs=[pl.BlockSpec((tm, d), row), _layer_spec(g1, layer),
                  pl.BlockSpec((None, d, tf), lambda i, f: (layer, 0, f)),
                  pl.BlockSpec((None, tf, d), lambda i, f: (layer, f, 0)),
                  _layer_spec(g2, layer)],
        out_specs=pl.BlockSpec((tm, d), row),
        out_shape=jax.ShapeDtypeStruct((t, d), F32),
        scratch_shapes=[pltpu.VMEM((tm, d), BF16), pltpu.VMEM((tm, d), F32)],
        compiler_params=_params(2),
        name="ffn",
    )(x, g1, wup, wdn, g2)


def _rope_tables(pos, rope):
    half = rope // 2
    inv_freq = jnp.exp(-math.log(ROPE_THETA) * 2.0 * jnp.arange(half, dtype=F32) / rope)
    ang = pos.astype(F32)[:, None] * inv_freq[None, :]
    cos, sin = jnp.cos(ang), jnp.sin(ang)
    pad = jnp.zeros((pos.shape[0], LANE - rope), F32)
    return jnp.concatenate([cos, cos, pad], axis=1), jnp.concatenate([-sin, sin, pad], axis=1)


def _swap_halves(w):
    half = w.shape[-1] // 2
    return jnp.concatenate([w[..., half:], w[..., :half]], axis=-1)


def _pad_lanes(w):
    return jnp.pad(w, [(0, 0)] * (w.ndim - 1) + [(0, LANE - w.shape[-1])])


def _tile(n, pref):
    return pref if n % pref == 0 else n


def kernel(x_prompt, x_sample, cache_kv_latent, cache_k_rope, state_ssm_re, state_ssm_im, page_table,
           g_pre_mix, w_in, g_q_lat, w_uq, g_kv_lat, w_uk, w_uv, lam_re, lam_im, log_dt, b_re, b_im,
           c_re, c_im, d_skip, w_glu, g_attn_out, g_ssm_out, w_out, g_post_mix, g_pre_mlp, w_up,
           w_down, g_post_mlp):
    bsz, seq, d = x_prompt.shape
    n_seq, dec_seq, _ = x_sample.shape
    assert dec_seq == 1
    depth = w_in.shape[0]
    q_lora = g_q_lat.shape[1]
    kv_lora, n_heads, nope = w_uk.shape[1:]
    v_dim = w_uv.shape[3]
    rope = cache_k_rope.shape[3]
    ssm_w = g_ssm_out.shape[1]
    n_grp, n_state = lam_re.shape[1:]
    gp = n_grp * n_state
    page = cache_kv_latent.shape[2]
    past_len = page_table.shape[1] * page
    assert nope == LANE and v_dim == LANE and rope * 2 == LANE
    scale = (nope + rope) ** -0.5
    seg_len = seq // N_SEG

    cos_p, sin_p = _rope_tables(jnp.arange(seq, dtype=jnp.int32), rope)
    cos_s, sin_s = _rope_tables(jnp.full((n_seq,), past_len, jnp.int32), rope)
    cache_rt = jnp.swapaxes(cache_k_rope, 2, 3)

    o_r, o_u = q_lora + kv_lora, q_lora + kv_lora + rope
    w_kr = w_in[:, :, o_r:o_u]
    w_in_b = jnp.concatenate(
        [w_in[:, :, :o_r], w_in[:, :, o_u:], _pad_lanes(w_kr), _pad_lanes(_swap_halves(w_kr))],
        axis=2).astype(BF16)
    wq = w_uq.reshape(depth, q_lora, n_heads, nope + rope)
    wq_r = wq[..., nope:]
    w_uq_b = jnp.concatenate(
        [wq[..., :nope].reshape(depth, q_lora, -1), _pad_lanes(wq_r).reshape(depth, q_lora, -1),
         _pad_lanes(_swap_halves(wq_r)).reshape(depth, q_lora, -1)], axis=2).astype(BF16)
    w_uk_b = w_uk.reshape(depth, kv_lora, n_heads * nope).astype(BF16)
    w_uv_b = w_uv.reshape(depth, kv_lora, n_heads * v_dim).astype(BF16)
    w_glu_b = w_glu.astype(BF16)
    w_out_b = w_out.astype(BF16)
    w_up_b = w_up.astype(BF16)
    w_dn_b = w_down.astype(BF16)
    gain = lambda a: a.reshape(depth, 1, -1)
    g_pre_mix, g_q_lat, g_kv_lat, g_attn_out, g_ssm_out, g_post_mix, g_pre_mlp, g_post_mlp = map(
        gain, (g_pre_mix, g_q_lat, g_kv_lat, g_attn_out, g_ssm_out, g_post_mix, g_pre_mlp, g_post_mlp))
    d_all = gain(d_skip)

    ab_re, ab_im, ap_re, ap_im, bb_re, bb_im = _s5_disc(lam_re, lam_im, log_dt, b_re, b_im, seg_len=seg_len)
    wbr, wbi, cdr, cdi = _s5_dense(bb_re, bb_im, c_re, c_im)
    a_re = jnp.broadcast_to(ab_re.reshape(depth, 1, gp), (depth, N_SEG, gp))
    a_im = jnp.broadcast_to(ab_im.reshape(depth, 1, gp), (depth, N_SEG, gp))
    ap_re = ap_re.reshape(depth, 1, gp)
    ap_im = ap_im.reshape(depth, 1, gp)
    h0_re = state_ssm_re.reshape(depth, n_seq, gp)
    h0_im = state_ssm_im.reshape(depth, n_seq, gp)

    tm_p = _tile(seq, 512)
    tf = _tile(w_up.shape[2], 1024)
    xp = x_prompt.reshape(bsz * seq, d)
    xs = x_sample.reshape(n_seq, d)
    in_kw = dict(q_lora=q_lora, kv_lora=kv_lora, ssm_w=ssm_w, rope=rope)

    outs = [[] for _ in range(8)]
    for l in range(depth):
        qlat, ckv, ckvb, kr, krb, u = _in_proj(
            xp, g_pre_mix, w_in_b, g_q_lat, g_kv_lat, cos_p, sin_p, layer=l, tm=tm_p, **in_kw)
        q, k, v = _qkv_prompt(qlat, ckvb, krb, w_uq_b, w_uk_b, w_uv_b, cos_p, sin_p,
                              layer=l, tm=tm_p, n_heads=n_heads, scale=scale)
        attn = _flash(q, k, v, bsz=bsz, seq=seq, n_heads=n_heads, blk=_tile(seq, 512))
        ssm, hf_r, hf_i = _s5_prompt(u, wbr, wbi, a_re, a_im, ap_re, ap_im, cdr, cdi, d_all, w_glu_b,
                                     layer=l, bsz=bsz, seq=seq, sb=_tile(seg_len, 32))
        x1 = _out_proj(xp, attn, ssm, g_attn_out, g_ssm_out, w_out_b, g_post_mix, layer=l, tm=tm_p)
        xp = _ffn(x1, g_pre_mlp, w_up_b, w_dn_b, g_post_mlp, layer=l, tm=tm_p, tf=tf)
        outs[0].append(ckv.reshape(bsz, seq, kv_lora))
        outs[1].append(kr.reshape(bsz, seq, rope))
        outs[2].append(hf_r.reshape(bsz, n_grp, n_state))
        outs[3].append(hf_i.reshape(bsz, n_grp, n_state))

        qlat, ckv, ckvb, kr, krb, u = _in_proj(
            xs, g_pre_mix, w_in_b, g_q_lat, g_kv_lat, cos_s, sin_s, layer=l, tm=n_seq, **in_kw)
        qs = _q_sample(qlat, w_uq_b, w_uk_b, cos_s, sin_s, layer=l, n_heads=n_heads, scale=scale)
        o_lat = _paged(page_table, qs.reshape(n_seq, n_heads, kv_lora + LANE),
                       ckv.reshape(n_seq, 1, kv_lora), _pad_lanes(kr).reshape(n_seq, 1, LANE),
                       cache_kv_latent, cache_rt, layer=l, n_heads=n_heads)
        attn = _uv_proj(o_lat.reshape(n_seq, n_heads * kv_lora), w_uv_b, layer=l, n_heads=n_heads)
        ssm, h_r, h_i = _s5_sample(u, h0_re, h0_im, wbr, wbi, a_re, a_im, cdr, cdi, d_all, w_glu_b, layer=l)
        x1 = _out_proj(xs, attn, ssm, g_attn_out, g_ssm_out, w_out_b, g_post_mix, layer=l, tm=n_seq)
        xs = _ffn(x1, g_pre_mlp, w_up_b, w_dn_b, g_post_mlp, layer=l, tm=n_seq, tf=tf)
        outs[4].append(ckv.reshape(n_seq, 1, kv_lora))
        outs[5].append(kr.reshape(n_seq, 1, rope))
        outs[6].append(h_r.reshape(n_seq, n_grp, n_state))
        outs[7].append(h_i.reshape(n_seq, n_grp, n_state))

    return (xp.reshape(bsz, seq, d), xs.reshape(n_seq, 1, d)) + tuple(jnp.stack(o) for o in outs)
```

```python
import functools
import math

import jax
import jax.numpy as jnp
from jax import lax
from jax.experimental import pallas as pl
from jax.experimental.pallas import tpu as pltpu

F32 = jnp.float32
BF16 = jnp.bfloat16

EPS = 1e-6
ROPE_THETA = 10000.0
LANE = 128
SUBLANE = 8
MXU_DIM = 256
VMEM_LIMIT = 56 * 1024 * 1024
N_SEG = SUBLANE
SCAN_LANES = 8 * LANE


def _params(n_axes):
    return pltpu.CompilerParams(dimension_semantics=("arbitrary",) * n_axes,
                                vmem_limit_bytes=VMEM_LIMIT)


def _const_spec(shape):
    zeros = (0,) * len(shape)
    return pl.BlockSpec(shape, lambda *_: zeros)


def _layer_spec(arr, layer):
    tail = tuple(arr.shape[1:])
    zeros = (0,) * len(tail)
    return pl.BlockSpec((None,) + tail, lambda *_: (layer,) + zeros)


def _rms(x, g):
    return x * lax.rsqrt(jnp.mean(x * x, axis=-1, keepdims=True) + EPS) * g


def _dot(a, b):
    return jnp.dot(a, b, preferred_element_type=F32)


def _dot_nt(a, b):
    return lax.dot_general(a, b, (((1,), (1,)), ((), ())), preferred_element_type=F32)


def _in_proj_kernel(x_ref, g_ref, w_ref, gq_ref, gkv_ref, cos_ref, sin_ref,
                    qlat_ref, ckv_ref, ckvb_ref, kr_ref, krb_ref, u_ref,
                    *, q_lora, kv_lora, ssm_w, rope):
    h = _rms(x_ref[...], g_ref[...]).astype(BF16)
    z = _dot(h, w_ref[...])
    o_kv = q_lora
    o_u = o_kv + kv_lora
    o_r = o_u + ssm_w
    qlat_ref[...] = _rms(z[:, :o_kv], gq_ref[...]).astype(BF16)
    ckv = _rms(z[:, o_kv:o_u], gkv_ref[...])
    ckv_ref[...] = ckv
    ckvb_ref[...] = ckv.astype(BF16)
    u_ref[...] = z[:, o_u:o_r]
    rot = z[:, o_r:o_r + LANE] * cos_ref[...] + z[:, o_r + LANE:o_r + 2 * LANE] * sin_ref[...]
    kr_ref[...] = rot[:, :rope]
    krb_ref[...] = rot.astype(BF16)


def _in_proj(x, g, w, gq, gkv, cos, sin, *, layer, tm, q_lora, kv_lora, ssm_w, rope):
    t, d = x.shape
    n_pos_blk = cos.shape[0] // tm
    row = lambda i: (i, 0)
    pos = lambda i: (i % n_pos_blk, 0)
    return pl.pallas_call(
        functools.partial(_in_proj_kernel, q_lora=q_lora, kv_lora=kv_lora, ssm_w=ssm_w, rope=rope),
        grid=(t // tm,),
        in_specs=[pl.BlockSpec((tm, d), row), _layer_spec(g, layer), _layer_spec(w, layer),
                  _layer_spec(gq, layer), _layer_spec(gkv, layer),
                  pl.BlockSpec((tm, LANE), pos), pl.BlockSpec((tm, LANE), pos)],
        out_specs=[pl.BlockSpec((tm, q_lora), row), pl.BlockSpec((tm, kv_lora), row),
                   pl.BlockSpec((tm, kv_lora), row), pl.BlockSpec((tm, rope), row),
                   pl.BlockSpec((tm, LANE), row), pl.BlockSpec((tm, ssm_w), row)],
        out_shape=[jax.ShapeDtypeStruct((t, q_lora), BF16), jax.ShapeDtypeStruct((t, kv_lora), F32),
                   jax.ShapeDtypeStruct((t, kv_lora), BF16), jax.ShapeDtypeStruct((t, rope), F32),
                   jax.ShapeDtypeStruct((t, LANE), BF16), jax.ShapeDtypeStruct((t, ssm_w), F32)],
        compiler_params=_params(1),
        name="in_proj",
    )(x, g, w, gq, gkv, cos, sin)


def _qkv_prompt_kernel(qlat_ref, ckvb_ref, krb_ref, wuq_ref, wuk_ref, wuv_ref, cos_ref, sin_ref,
                       q_ref, k_ref, v_ref, *, n_heads, scale):
    z = _dot(qlat_ref[...], wuq_ref[...])
    ckvb = ckvb_ref[...]
    kn = _dot(ckvb, wuk_ref[...])
    v_ref[...] = _dot(ckvb, wuv_ref[...]).astype(BF16)
    cos = cos_ref[...]
    sin = sin_ref[...]
    krb = krb_ref[...]
    o_r = n_heads * LANE
    o_s = 2 * n_heads * LANE
    for h in range(n_heads):
        lo = h * LANE
        hd = h * MXU_DIM
        q_ref[:, hd:hd + LANE] = (z[:, lo:lo + LANE] * scale).astype(BF16)
        rot = z[:, o_r + lo:o_r + lo + LANE] * cos + z[:, o_s + lo:o_s + lo + LANE] * sin
        q_ref[:, hd + LANE:hd + 2 * LANE] = (rot * scale).astype(BF16)
        k_ref[:, hd:hd + LANE] = kn[:, lo:lo + LANE].astype(BF16)
        k_ref[:, hd + LANE:hd + 2 * LANE] = krb


def _qkv_prompt(qlat, ckvb, krb, wuq, wuk, wuv, cos, sin, *, layer, tm, n_heads, scale):
    t, q_lora = qlat.shape
    kv_lora = ckvb.shape[1]
    n_pos_blk = cos.shape[0] // tm
    row = lambda i: (i, 0)
    pos = lambda i: (i % n_pos_blk, 0)
    hw = n_heads * MXU_DIM
    vw = wuv.shape[2]
    return pl.pallas_call(
        functools.partial(_qkv_prompt_kernel, n_heads=n_heads, scale=scale),
        grid=(t // tm,),
        in_specs=[pl.BlockSpec((tm, q_lora), row), pl.BlockSpec((tm, kv_lora), row),
                  pl.BlockSpec((tm, LANE), row), _layer_spec(wuq, layer), _layer_spec(wuk, layer),
                  _layer_spec(wuv, layer), pl.BlockSpec((tm, LANE), pos), pl.BlockSpec((tm, LANE), pos)],
        out_specs=[pl.BlockSpec((tm, hw), row), pl.BlockSpec((tm, hw), row), pl.BlockSpec((tm, vw), row)],
        out_shape=[jax.ShapeDtypeStruct((t, hw), BF16), jax.ShapeDtypeStruct((t, hw), BF16),
                   jax.ShapeDtypeStruct((t, vw), BF16)],
        compiler_params=_params(1),
        name="qkv_prompt",
    )(qlat, ckvb, krb, wuq, wuk, wuv, cos, sin)


def _q_sample_kernel(qlat_ref, wuq_ref, wuk_ref, cos_ref, sin_ref, q_ref, *, n_heads, kv_lora, scale):
    z = _dot(qlat_ref[...], wuq_ref[...])
    cos = cos_ref[...]
    sin = sin_ref[...]
    o_r = n_heads * LANE
    o_s = 2 * n_heads * LANE
    hw = kv_lora + LANE
    for h in range(n_heads):
        lo = h * LANE
        qn = z[:, lo:lo + LANE].astype(BF16)
        q_abs = _dot_nt(qn, wuk_ref[:, lo:lo + LANE])
        q_ref[:, h * hw:h * hw + kv_lora] = (q_abs * scale).astype(BF16)
        rot = z[:, o_r + lo:o_r + lo + LANE] * cos + z[:, o_s + lo:o_s + lo + LANE] * sin
        q_ref[:, h * hw + kv_lora:(h + 1) * hw] = (rot * scale).astype(BF16)


def _q_sample(qlat, wuq, wuk, cos, sin, *, layer, n_heads, scale):
    t = qlat.shape[0]
    kv_lora = wuk.shape[1]
    hw = kv_lora + LANE
    return pl.pallas_call(
        functools.partial(_q_sample_kernel, n_heads=n_heads, kv_lora=kv_lora, scale=scale),
        grid=(1,),
        in_specs=[_const_spec(qlat.shape), _layer_spec(wuq, layer), _layer_spec(wuk, layer),
                  _const_spec(cos.shape), _const_spec(sin.shape)],
        out_specs=_const_spec((t, n_heads * hw)),
        out_shape=jax.ShapeDtypeStruct((t, n_heads * hw), BF16),
        compiler_params=_params(1),
        name="q_sample",
    )(qlat, wuq, wuk, cos, sin)


HEADS_PER_STEP = 4


def _flash_kernel(q_ref, k_ref, v_ref, o_ref, m_ref, acc_ref, s_ref, *, blk, hps):
    i = pl.program_id(2)
    m_ref[...] = jnp.full(m_ref.shape, -jnp.inf, F32)
    acc_ref[...] = jnp.zeros(acc_ref.shape, F32)
    ones = jnp.ones((blk, LANE), BF16)
    rows = lax.broadcasted_iota(jnp.int32, (blk, blk), 0)
    cols = lax.broadcasted_iota(jnp.int32, (blk, blk), 1)
    visible = cols <= rows

    def issue_scores(slot, j):
        ks = pl.multiple_of(j * blk, blk)
        for h in range(hps):
            hc = slice(h * MXU_DIM, (h + 1) * MXU_DIM)
            s_ref[slot, h] = _dot_nt(q_ref[:, hc], k_ref[pl.ds(ks, blk), hc])

    def consume(slot, j, diagonal):
        ks = pl.multiple_of(j * blk, blk)
        for h in range(hps):
            s = s_ref[slot, h]
            if diagonal:
                s = jnp.where(visible, s, -jnp.inf)
            v1 = jnp.concatenate([v_ref[pl.ds(ks, blk), h * LANE:(h + 1) * LANE], ones], axis=-1)
            m_prev = m_ref[h]
            m_new = jnp.maximum(m_prev, jnp.max(s, axis=-1, keepdims=True))
            p = jnp.exp(s - m_new).astype(BF16)
            acc_ref[h] = jnp.exp(m_prev - m_new) * acc_ref[h] + _dot(p, v1)
            m_ref[h] = m_new

    issue_scores(0, 0)

    def block_pair(jj, carry):
        j = 2 * jj
        issue_scores(1, j + 1)
        consume(0, j, False)
        issue_scores(0, j + 2)
        consume(1, j + 1, False)
        return carry

    lax.fori_loop(0, i >> 1, block_pair, 0)

    @pl.when((i & 1) == 1)
    def _():
        issue_scores(1, i)
        consume(0, i - 1, False)
        consume(1, i, True)

    @pl.when((i & 1) == 0)
    def _():
        consume(0, i, True)

    for h in range(hps):
        acc = acc_ref[h]
        o_ref[:, h * LANE:(h + 1) * LANE] = acc[:, :LANE] / acc[:, LANE:]


def _flash(q, k, v, *, bsz, seq, n_heads, blk):
    v_dim = v.shape[1] // n_heads
    assert v_dim == LANE
    hps = HEADS_PER_STEP if n_heads % HEADS_PER_STEP == 0 else 1
    nq = seq // blk
    return pl.pallas_call(
        functools.partial(_flash_kernel, blk=blk, hps=hps),
        grid=(bsz, n_heads // hps, nq),
        in_specs=[pl.BlockSpec((blk, hps * MXU_DIM), lambda b, h, i: (b * nq + i, h)),
                  pl.BlockSpec((seq, hps * MXU_DIM), lambda b, h, i: (b, h)),
                  pl.BlockSpec((seq, hps * v_dim), lambda b, h, i: (b, h))],
        out_specs=pl.BlockSpec((blk, hps * v_dim), lambda b, h, i: (b * nq + i, h)),
        out_shape=jax.ShapeDtypeStruct((bsz * seq, n_heads * v_dim), F32),
        scratch_shapes=[pltpu.VMEM((hps, blk, 1), F32), pltpu.VMEM((hps, blk, 2 * LANE), F32),
                        pltpu.VMEM((2, hps, blk, blk), F32)],
        compiler_params=_params(3),
        name="flash_prompt",
    )(q, k, v)


def _paged_kernel(pt_ref, q_ref, cnew_ref, rnew_ref, cc_hbm, cr_hbm, o_ref,
                  cbuf, rbuf, sem_c, sem_r, *, layer, n_pages, page, kv_lora, rope):
    b = pl.program_id(0)
    slot = b & 1

    def page_copies(seq, sl, p):
        pid = pt_ref[seq, p]
        return (pltpu.make_async_copy(cc_hbm.at[layer, pid],
                                      cbuf.at[sl, pl.ds(p * page, page), :], sem_c.at[sl]),
                pltpu.make_async_copy(cr_hbm.at[layer, pid],
                                      rbuf.at[sl, :, pl.ds(p * page, page)], sem_r.at[sl]))

    def start_sequence(seq, sl):
        for p in range(n_pages):
            for cp in page_copies(seq, sl, p):
                cp.start()

    @pl.when(b == 0)
    def _():
        start_sequence(0, 0)

    @pl.when(b + 1 < pl.num_programs(0))
    def _():
        start_sequence(b + 1, 1 - slot)

    for p in range(n_pages):
        for cp in page_copies(b, slot, p):
            cp.wait()

    cb = cbuf[slot].astype(BF16)
    rb = rbuf[slot].astype(BF16)
    q = q_ref[0]
    s = _dot_nt(q[:, :kv_lora], cb) + _dot(q[:, kv_lora:kv_lora + rope], rb)
    cn = cnew_ref[0]
    qf = q.astype(F32)
    s_new = (jnp.sum(qf[:, :kv_lora] * cn, axis=-1, keepdims=True)
             + jnp.sum(qf[:, kv_lora:] * rnew_ref[0], axis=-1, keepdims=True))
    m = jnp.maximum(jnp.max(s, axis=-1, keepdims=True), s_new)
    p_old = jnp.exp(s - m)
    p_new = jnp.exp(s_new - m)
    denom = jnp.sum(p_old, axis=-1, keepdims=True) + p_new
    pb = p_old.astype(BF16)
    half = pb.shape[1] // 2
    pv = _dot(pb[:, :half], cb[:half]) + _dot(pb[:, half:], cb[half:])
    o_ref[0] = (pv + p_new * cn) / denom


def _paged(page_table, q, cnew, rnew, cache_c, cache_rt, *, layer, n_heads):
    n_seq, n_pages = page_table.shape
    _, _, page, kv_lora = cache_c.shape
    rope = cache_rt.shape[2]
    hw = q.shape[2]
    n_keys = n_pages * page
    seq3 = lambda b, pt: (b, 0, 0)
    grid_spec = pltpu.PrefetchScalarGridSpec(
        num_scalar_prefetch=1,
        grid=(n_seq,),
        in_specs=[pl.BlockSpec((1, n_heads, hw), seq3), pl.BlockSpec((1, 1, kv_lora), seq3),
                  pl.BlockSpec((1, 1, LANE), seq3),
                  pl.BlockSpec(memory_space=pl.ANY), pl.BlockSpec(memory_space=pl.ANY)],
        out_specs=pl.BlockSpec((1, n_heads, kv_lora), seq3),
        scratch_shapes=[pltpu.VMEM((2, n_keys, kv_lora), F32), pltpu.VMEM((2, rope, n_keys), F32),
                        pltpu.SemaphoreType.DMA((2,)), pltpu.SemaphoreType.DMA((2,))],
    )
    return pl.pallas_call(
        functools.partial(_paged_kernel, layer=layer, n_pages=n_pages, page=page,
                          kv_lora=kv_lora, rope=rope),
        grid_spec=grid_spec,
        out_shape=jax.ShapeDtypeStruct((n_seq, n_heads, kv_lora), F32),
        compiler_params=_params(1),
        name="paged_sample",
    )(page_table, q, cnew, rnew, cache_c, cache_rt)


def _uv_kernel(ol_ref, wuv_ref, o_ref, *, n_heads, kv_lora):
    for h in range(n_heads):
        o_ref[:, h * LANE:(h + 1) * LANE] = _dot(
            ol_ref[:, h * kv_lora:(h + 1) * kv_lora].astype(BF16), wuv_ref[:, h * LANE:(h + 1) * LANE])


def _uv_proj(o_lat, wuv, *, layer, n_heads):
    t = o_lat.shape[0]
    kv_lora, vw = wuv.shape[1:]
    return pl.pallas_call(
        functools.partial(_uv_kernel, n_heads=n_heads, kv_lora=kv_lora),
        grid=(1,),
        in_specs=[_const_spec(o_lat.shape), _layer_spec(wuv, layer)],
        out_specs=_const_spec((t, vw)),
        out_shape=jax.ShapeDtypeStruct((t, vw), F32),
        compiler_params=_params(1),
        name="uv_sample",
    )(o_lat, wuv)


def _s5_disc_kernel(lr_ref, li_ref, ldt_ref, bre_ref, bim_ref,
                    abr_ref, abi_ref, apr_ref, api_ref, bbr_ref, bbi_ref, *, n_sq):
    lr = lr_ref[...]
    li = li_ref[...]
    dt = jnp.exp(ldt_ref[...])
    mag = jnp.exp(lr * dt)
    ab_re = mag * jnp.cos(li * dt)
    ab_im = mag * jnp.sin(li * dt)
    den = lr * lr + li * li
    nr = ab_re - 1.0
    f_re = (nr * lr + ab_im * li) / den
    f_im = (ab_im * lr - nr * li) / den
    abr_ref[...] = ab_re
    abi_ref[...] = ab_im
    pr, pi = ab_re, ab_im
    for _ in range(n_sq):
        pr, pi = pr * pr - pi * pi, 2.0 * pr * pi
    apr_ref[...] = pr
    api_ref[...] = pi
    for h in range(bre_ref.shape[0]):
        b_re = bre_ref[h]
        b_im = bim_ref[h]
        bbr_ref[h] = f_re * b_re - f_im * b_im
        bbi_ref[h] = f_re * b_im + f_im * b_re


def _s5_disc(lam_re, lam_im, log_dt, b_re, b_im, *, seg_len):
    depth, g, p = lam_re.shape
    ch = b_re.shape[3]
    n_sq = int(math.log2(seg_len))
    assert 2 ** n_sq == seg_len
    b_re_t = jnp.transpose(b_re, (0, 3, 1, 2))
    b_im_t = jnp.transpose(b_im, (0, 3, 1, 2))
    gp_spec = pl.BlockSpec((None, g, p), lambda l: (l, 0, 0))
    hgp_spec = pl.BlockSpec((None, ch, g, p), lambda l: (l, 0, 0, 0))
    gp = jax.ShapeDtypeStruct((depth, g, p), F32)
    hgp = jax.ShapeDtypeStruct((depth, ch, g, p), F32)
    return pl.pallas_call(
        functools.partial(_s5_disc_kernel, n_sq=n_sq),
        grid=(depth,),
        in_specs=[gp_spec, gp_spec, pl.BlockSpec((None, g, 1), lambda l: (l, 0, 0)), hgp_spec, hgp_spec],
        out_specs=[gp_spec] * 4 + [hgp_spec] * 2,
        out_shape=[gp, gp, gp, gp, hgp, hgp],
        compiler_params=_params(1),
        name="s5_disc",
    )(lam_re, lam_im, log_dt.reshape(depth, g, 1), b_re_t, b_im_t)


def _s5_dense(bb_re, bb_im, c_re, c_im):
    depth, ch, g, p = bb_re.shape
    gk = MXU_DIM // ch
    go = LANE // ch

    def in_side(bb):
        m = jnp.transpose(bb, (0, 2, 1, 3)).reshape(depth, g // gk, gk, ch, p)
        eye = jnp.eye(gk, dtype=F32)
        return jnp.einsum('lkghp,gj->lkghjp', m, eye).reshape(depth, g // gk, gk * ch, gk * p).astype(BF16)

    def out_side(c):
        m = jnp.transpose(c, (0, 1, 3, 2)).reshape(depth, g // go, go, p, ch)
        eye = jnp.eye(go, dtype=F32)
        return jnp.einsum('lkgph,gj->lkgpjh', m, eye).reshape(depth, g // go, go * p, go * ch).astype(BF16)

    return in_side(bb_re), in_side(bb_im), out_side(c_re), out_side(-c_im)


def _s5_readout(hr, hi, u, cdr_ref, cdi_ref, d_ref, wglu_ref):
    n_ob, kc, _ = cdr_ref.shape
    hrb = hr.astype(BF16)
    hib = hi.astype(BF16)
    y = jnp.concatenate(
        [_dot(hrb[:, ob * kc:(ob + 1) * kc], cdr_ref[ob]) + _dot(hib[:, ob * kc:(ob + 1) * kc], cdi_ref[ob])
         for ob in range(n_ob)], axis=-1)
    y = jax.nn.gelu(y + d_ref[...] * u).astype(BF16)
    z = _dot(y, wglu_ref[...])
    w = z.shape[1] // 2
    return z[:, :w] * jax.nn.sigmoid(z[:, w:])


def _s5_project_in(lhs, wbr_ref, wbi_ref, xr_ref, xi_ref, row0=0):
    n_kt, kc, nc = wbr_ref.shape
    rows = slice(row0, row0 + lhs.shape[0])
    for kt in range(n_kt):
        piece = lhs[:, kt * kc:(kt + 1) * kc]
        xr_ref[rows, kt * nc:(kt + 1) * nc] = _dot(piece, wbr_ref[kt])
        xi_ref[rows, kt * nc:(kt + 1) * nc] = _dot(piece, wbi_ref[kt])


def _s5_prompt_kernel(u_ref, wbr_ref, wbi_ref, ar_ref, ai_ref, apr_ref, api_ref,
                      cdr_ref, cdi_ref, d_ref, wglu_ref,
                      ssm_ref, hfr_ref, hfi_ref,
                      uperm_ref, xr_ref, xi_ref, *, sb):
    ps = pl.program_id(1)
    blk = pl.program_id(2)
    rows = sb * N_SEG
    gp = xr_ref.shape[1]
    head = slice(0, N_SEG)
    tail = slice(rows, rows + N_SEG)

    @pl.when((ps == 0) & (blk == 0))
    def _():
        xr_ref[head, :] = jnp.zeros((N_SEG, gp), F32)
        xi_ref[head, :] = jnp.zeros((N_SEG, gp), F32)

    @pl.when((ps == 1) & (blk == 0))
    def _():
        apr = apr_ref[...]
        api = api_ref[...]
        in_r = jnp.zeros(apr.shape, F32)
        in_i = jnp.zeros(apr.shape, F32)
        for seg in range(N_SEG):
            loc_r = xr_ref[seg:seg + 1, :]
            loc_i = xi_ref[seg:seg + 1, :]
            xr_ref[seg:seg + 1, :] = in_r
            xi_ref[seg:seg + 1, :] = in_i
            in_r, in_i = (loc_r + apr * in_r - api * in_i, loc_i + apr * in_i + api * in_r)

    for s in range(sb):
        uperm_ref[s * N_SEG:(s + 1) * N_SEG, :] = u_ref[0, :, s, :]
    _s5_project_in(uperm_ref[...].astype(BF16), wbr_ref, wbi_ref, xr_ref, xi_ref, row0=N_SEG)

    strip = min(SCAN_LANES, gp)
    for c in range(gp // strip):
        cols = slice(c * strip, (c + 1) * strip)
        ar = ar_ref[:, cols]
        ai = ai_ref[:, cols]

        def step(s, carry, cols=cols, ar=ar, ai=ai):
            hr, hi = carry
            r1 = pl.multiple_of(s * N_SEG + N_SEG, N_SEG)
            nr = ar * hr - ai * hi + xr_ref[pl.ds(r1, N_SEG), cols]
            ni = ar * hi + ai * hr + xi_ref[pl.ds(r1, N_SEG), cols]
            xr_ref[pl.ds(r1, N_SEG), cols] = nr
            xi_ref[pl.ds(r1, N_SEG), cols] = ni
            return nr, ni

        lax.fori_loop(0, sb, step, (xr_ref[head, cols], xi_ref[head, cols]))

    @pl.when(ps == 1)
    def _():
        out = _s5_readout(xr_ref[N_SEG:N_SEG + rows, :], xi_ref[N_SEG:N_SEG + rows, :], uperm_ref[...],
                          cdr_ref, cdi_ref, d_ref, wglu_ref)
        for s in range(sb):
            ssm_ref[0, :, s, :] = out[s * N_SEG:(s + 1) * N_SEG, :]
        hfr_ref[0] = xr_ref[tail, :]
        hfi_ref[0] = xi_ref[tail, :]

    xr_ref[head, :] = xr_ref[tail, :]
    xi_ref[head, :] = xi_ref[tail, :]


def _s5_prompt(u, wbr, wbi, a_re, a_im, ap_re, ap_im, cdr, cdi, d_skip, wglu, *, layer, bsz, seq, sb):
    width = u.shape[1]
    gp = a_re.shape[2]
    seg_len = seq // N_SEG
    nblk = seg_len // sb
    rows = sb * N_SEG
    u4 = u.reshape(bsz, N_SEG, seg_len, width)
    blk4 = lambda b, ps, k: (b, 0, k, 0)
    weights = (wbr, wbi, a_re, a_im, ap_re, ap_im, cdr, cdi, d_skip, wglu)
    ssm, hf_r, hf_i = pl.pallas_call(
        functools.partial(_s5_prompt_kernel, sb=sb),
        grid=(bsz, 2, nblk),
        in_specs=[pl.BlockSpec((1, N_SEG, sb, width), blk4)] + [_layer_spec(w, layer) for w in weights],
        out_specs=[pl.BlockSpec((1, N_SEG, sb, width), lambda b, ps, k: (b, 0, k * ps, 0)),
                   pl.BlockSpec((1, N_SEG, gp), lambda b, ps, k: (b, 0, 0)),
                   pl.BlockSpec((1, N_SEG, gp), lambda b, ps, k: (b, 0, 0))],
        out_shape=[jax.ShapeDtypeStruct((bsz, N_SEG, seg_len, width), F32),
                   jax.ShapeDtypeStruct((bsz, N_SEG, gp), F32),
                   jax.ShapeDtypeStruct((bsz, N_SEG, gp), F32)],
        scratch_shapes=[pltpu.VMEM((rows, width), F32),
                        pltpu.VMEM((rows + N_SEG, gp), F32), pltpu.VMEM((rows + N_SEG, gp), F32)],
        compiler_params=_params(3),
        name="s5_prompt",
    )(u4, *weights)
    return ssm.reshape(bsz * seq, width), hf_r[:, N_SEG - 1], hf_i[:, N_SEG - 1]


def _s5_sample_kernel(u_ref, h0r_ref, h0i_ref, wbr_ref, wbi_ref, ar_ref, ai_ref,
                      cdr_ref, cdi_ref, d_ref, wglu_ref,
                      ssm_ref, hr_ref, hi_ref, xr_ref, xi_ref):
    u = u_ref[...]
    _s5_project_in(u.astype(BF16), wbr_ref, wbi_ref, xr_ref, xi_ref)
    ar = ar_ref[0:1, :]
    ai = ai_ref[0:1, :]
    h0r = h0r_ref[...]
    h0i = h0i_ref[...]
    hr = ar * h0r - ai * h0i + xr_ref[...]
    hi = ar * h0i + ai * h0r + xi_ref[...]
    hr_ref[...] = hr
    hi_ref[...] = hi
    ssm_ref[...] = _s5_readout(hr, hi, u, cdr_ref, cdi_ref, d_ref, wglu_ref)


def _s5_sample(u, h0_re, h0_im, wbr, wbi, a_re, a_im, cdr, cdi, d_skip, wglu, *, layer):
    t, width = u.shape
    gp = a_re.shape[2]
    st = jax.ShapeDtypeStruct((t, gp), F32)
    weights = (wbr, wbi, a_re, a_im, cdr, cdi, d_skip, wglu)
    return pl.pallas_call(
        _s5_sample_kernel,
        grid=(1,),
        in_specs=([_const_spec((t, width)), _layer_spec(h0_re, layer), _layer_spec(h0_im, layer)]
                  + [_layer_spec(w, layer) for w in weights]),
        out_specs=[_const_spec((t, width)), _const_spec((t, gp)), _const_spec((t, gp))],
        out_shape=[jax.ShapeDtypeStruct((t, width), F32), st, st],
        scratch_shapes=[pltpu.VMEM((t, gp), F32), pltpu.VMEM((t, gp), F32)],
        compiler_params=_params(1),
        name="s5_sample",
    )(u, h0_re, h0_im, *weights)


def _out_proj_kernel(x_ref, attn_ref, ssm_ref, ga_ref, gs_ref, wa_ref, ws_ref, gp_ref, o_ref):
    na = _rms(attn_ref[...], ga_ref[...]).astype(BF16)
    ns = _rms(ssm_ref[...], gs_ref[...]).astype(BF16)
    mix = _dot(na, wa_ref[...]) + _dot(ns, ws_ref[...])
    o_ref[...] = x_ref[...] + _rms(mix, gp_ref[...])


def _out_proj(x, attn, ssm, ga, gs, w_out, gp, *, layer, tm):
    t, d = x.shape
    aw = attn.shape[1]
    sw = ssm.shape[1]
    assert aw == sw and w_out.shape[1] == aw + sw
    row = lambda i: (i, 0)
    return pl.pallas_call(
        _out_proj_kernel,
        grid=(t // tm,),
        in_specs=[pl.BlockSpec((tm, d), row), pl.BlockSpec((tm, aw), row), pl.BlockSpec((tm, sw), row),
                  _layer_spec(ga, layer), _layer_spec(gs, layer),
                  pl.BlockSpec((None, aw, d), lambda i: (layer, 0, 0)),
                  pl.BlockSpec((None, sw, d), lambda i: (layer, 1, 0)),
                  _layer_spec(gp, layer)],
        out_specs=pl.BlockSpec((tm, d), row),
        out_shape=jax.ShapeDtypeStruct((t, d), F32),
        compiler_params=_params(1),
        name="out_proj",
    )(x, attn, ssm, ga, gs, w_out, w_out, gp)


def _ffn_kernel(x_ref, g1_ref, wup_ref, wdn_ref, g2_ref, o_ref, hn_ref, acc_ref):
    f = pl.program_id(1)

    @pl.when(f == 0)
    def _():
        hn_ref[...] = _rms(x_ref[...], g1_ref[...]).astype(BF16)
        acc_ref[...] = jnp.zeros(acc_ref.shape, F32)

    a = jnp.square(jnp.maximum(_dot(hn_ref[...], wup_ref[...]), 0.0)).astype(BF16)
    acc_ref[...] += _dot(a, wdn_ref[...])

    @pl.when(f == pl.num_programs(1) - 1)
    def _():
        o_ref[...] = x_ref[...] + _rms(acc_ref[...], g2_ref[...])


def _ffn(x, g1, wup, wdn, g2, *, layer, tm, tf):
    t, d = x.shape
    d_ff = wup.shape[2]
    row = lambda i, f: (i, 0)
    return pl.pallas_call(
        _ffn_kernel,
        grid=(t // tm, d_ff // tf),
        in_specs=[pl.BlockSpec((tm, d), row), _layer_spec(g1, layer),
                  pl.BlockSpec((None, d, tf), lambda i, f: (layer, 0, f)),
                  pl.BlockSpec((None, tf, d), lambda i, f: (layer, f, 0)),
                  _layer_spec(g2, layer)],
        out_specs=pl.BlockSpec((tm, d), row),
        out_shape=jax.ShapeDtypeStruct((t, d), F32),
        scratch_shapes=[pltpu.VMEM((tm, d), BF16), pltpu.VMEM((tm, d), F32)],
        compiler_params=_params(2),
        name="ffn",
    )(x, g1, wup, wdn, g2)


def _rope_tables(pos, rope):
    half = rope // 2
    inv_freq = jnp.exp(-math.log(ROPE_THETA) * 2.0 * jnp.arange(half, dtype=F32) / rope)
    ang = pos.astype(F32)[:, None] * inv_freq[None, :]
    cos, sin = jnp.cos(ang), jnp.sin(ang)
    pad = jnp.zeros((pos.shape[0], LANE - rope), F32)
    return jnp.concatenate([cos, cos, pad], axis=1), jnp.concatenate([-sin, sin, pad], axis=1)


def _swap_halves(w):
    half = w.shape[-1] // 2
    return jnp.concatenate([w[..., half:], w[..., :half]], axis=-1)


def _pad_lanes(w):
    return jnp.pad(w, [(0, 0)] * (w.ndim - 1) + [(0, LANE - w.shape[-1])])


def _tile(n, pref):
    return pref if n % pref == 0 else n


def kernel(x_prompt, x_sample, cache_kv_latent, cache_k_rope, state_ssm_re, state_ssm_im, page_table,
           g_pre_mix, w_in, g_q_lat, w_uq, g_kv_lat, w_uk, w_uv, lam_re, lam_im, log_dt, b_re, b_im,
           c_re, c_im, d_skip, w_glu, g_attn_out, g_ssm_out, w_out, g_post_mix, g_pre_mlp, w_up,
           w_down, g_post_mlp):
    bsz, seq, d = x_prompt.shape
    n_seq, dec_seq, _ = x_sample.shape
    assert dec_seq == 1
    depth = w_in.shape[0]
    q_lora = g_q_lat.shape[1]
    kv_lora, n_heads, nope = w_uk.shape[1:]
    v_dim = w_uv.shape[3]
    rope = cache_k_rope.shape[3]
    ssm_w = g_ssm_out.shape[1]
    n_grp, n_state = lam_re.shape[1:]
    gp = n_grp * n_state
    page = cache_kv_latent.shape[2]
    past_len = page_table.shape[1] * page
    assert nope == LANE and v_dim == LANE and rope * 2 == LANE
    scale = (nope + rope) ** -0.5
    seg_len = seq // N_SEG

    cos_p, sin_p = _rope_tables(jnp.arange(seq, dtype=jnp.int32), rope)
    cos_s, sin_s = _rope_tables(jnp.full((n_seq,), past_len, jnp.int32), rope)
    cache_rt = jnp.swapaxes(cache_k_rope, 2, 3)

    o_r, o_u = q_lora + kv_lora, q_lora + kv_lora + rope
    w_kr = w_in[:, :, o_r:o_u]
    w_in_b = jnp.concatenate(
        [w_in[:, :, :o_r], w_in[:, :, o_u:], _pad_lanes(w_kr), _pad_lanes(_swap_halves(w_kr))],
        axis=2).astype(BF16)
    wq = w_uq.reshape(depth, q_lora, n_heads, nope + rope)
    wq_r = wq[..., nope:]
    w_uq_b = jnp.concatenate(
        [wq[..., :nope].reshape(depth, q_lora, -1), _pad_lanes(wq_r).reshape(depth, q_lora, -1),
         _pad_lanes(_swap_halves(wq_r)).reshape(depth, q_lora, -1)], axis=2).astype(BF16)
    w_uk_b = w_uk.reshape(depth, kv_lora, n_heads * nope).astype(BF16)
    w_uv_b = w_uv.reshape(depth, kv_lora, n_heads * v_dim).astype(BF16)
    w_glu_b = w_glu.astype(BF16)
    w_out_b = w_out.astype(BF16)
    w_up_b = w_up.astype(BF16)
    w_dn_b = w_down.astype(BF16)
    gain = lambda a: a.reshape(depth, 1, -1)
    g_pre_mix, g_q_lat, g_kv_lat, g_attn_out, g_ssm_out, g_post_mix, g_pre_mlp, g_post_mlp = map(
        gain, (g_pre_mix, g_q_lat, g_kv_lat, g_attn_out, g_ssm_out, g_post_mix, g_pre_mlp, g_post_mlp))
    d_all = gain(d_skip)

    ab_re, ab_im, ap_re, ap_im, bb_re, bb_im = _s5_disc(lam_re, lam_im, log_dt, b_re, b_im, seg_len=seg_len)
    wbr, wbi, cdr, cdi = _s5_dense(bb_re, bb_im, c_re, c_im)
    a_re = jnp.broadcast_to(ab_re.reshape(depth, 1, gp), (depth, N_SEG, gp))
    a_im = jnp.broadcast_to(ab_im.reshape(depth, 1, gp), (depth, N_SEG, gp))
    ap_re = ap_re.reshape(depth, 1, gp)
    ap_im = ap_im.reshape(depth, 1, gp)
    h0_re = state_ssm_re.reshape(depth, n_seq, gp)
    h0_im = state_ssm_im.reshape(depth, n_seq, gp)

    tm_p = _tile(seq, 512)
    tf = _tile(w_up.shape[2], 1024)
    xp = x_prompt.reshape(bsz * seq, d)
    xs = x_sample.reshape(n_seq, d)
    in_kw = dict(q_lora=q_lora, kv_lora=kv_lora, ssm_w=ssm_w, rope=rope)

    outs = [[] for _ in range(8)]
    for l in range(depth):
        qlat, ckv, ckvb, kr, krb, u = _in_proj(
            xp, g_pre_mix, w_in_b, g_q_lat, g_kv_lat, cos_p, sin_p, layer=l, tm=tm_p, **in_kw)
        q, k, v = _qkv_prompt(qlat, ckvb, krb, w_uq_b, w_uk_b, w_uv_b, cos_p, sin_p,
                              layer=l, tm=tm_p, n_heads=n_heads, scale=scale)
        attn = _flash(q, k, v, bsz=bsz, seq=seq, n_heads=n_heads, blk=_tile(seq, 512))
        ssm, hf_r, hf_i = _s5_prompt(u, wbr, wbi, a_re, a_im, ap_re, ap_im, cdr, cdi, d_all, w_glu_b,
                                     layer=l, bsz=bsz, seq=seq, sb=_tile(seg_len, 32))
        x1 = _out_proj(xp, attn, ssm, g_attn_out, g_ssm_out, w_out_b, g_post_mix, layer=l, tm=tm_p)
        xp = _ffn(x1, g_pre_mlp, w_up_b, w_dn_b, g_post_mlp, layer=l, tm=tm_p, tf=tf)
        outs[0].append(ckv.reshape(bsz, seq, kv_lora))
        outs[1].append(kr.reshape(bsz, seq, rope))
        outs[2].append(hf_r.reshape(bsz, n_grp, n_state))
        outs[3].append(hf_i.reshape(bsz, n_grp, n_state))

        qlat, ckv, ckvb, kr, krb, u = _in_proj(
            xs, g_pre_mix, w_in_b, g_q_lat, g_kv_lat, cos_s, sin_s, layer=l, tm=n_seq, **in_kw)
        qs = _q_sample(qlat, w_uq_b, w_uk_b, cos_s, sin_s, layer=l, n_heads=n_heads, scale=scale)
        o_lat = _paged(page_table, qs.reshape(n_seq, n_heads, kv_lora + LANE),
                       ckv.reshape(n_seq, 1, kv_lora), _pad_lanes(kr).reshape(n_seq, 1, LANE),
                       cache_kv_latent, cache_rt, layer=l, n_heads=n_heads)
        attn = _uv_proj(o_lat.reshape(n_seq, n_heads * kv_lora), w_uv_b, layer=l, n_heads=n_heads)
        ssm, h_r, h_i = _s5_sample(u, h0_re, h0_im, wbr, wbi, a_re, a_im, cdr, cdi, d_all, w_glu_b, layer=l)
        x1 = _out_proj(xs, attn, ssm, g_attn_out, g_ssm_out, w_out_b, g_post_mix, layer=l, tm=n_seq)
        xs = _ffn(x1, g_pre_mlp, w_up_b, w_dn_b, g_post_mlp, layer=l, tm=n_seq, tf=tf)
        outs[4].append(ckv.reshape(n_seq, 1, kv_lora))
        outs[5].append(kr.reshape(n_seq, 1, rope))
        outs[6].append(h_r.reshape(n_seq, n_grp, n_state))
        outs[7].append(h_i.reshape(n_seq, n_grp, n_state))

    return (xp.reshape(bsz, seq, d), xs.reshape(n_seq, 1, d)) + tuple(jnp.stack(o) for o in outs)
```

```python
import functools
import math

import jax
import jax.numpy as jnp
from jax import lax
from jax.experimental import pallas as pl
from jax.experimental.pallas import tpu as pltpu

F32 = jnp.float32
BF16 = jnp.bfloat16

EPS = 1e-6
ROPE_THETA = 10000.0
LANE = 128
SUBLANE = 8
MXU_DIM = 256
VMEM_LIMIT = 56 * 1024 * 1024
N_SEG = SUBLANE


def _params(n_axes):
    return pltpu.CompilerParams(dimension_semantics=("arbitrary",) * n_axes,
                                vmem_limit_bytes=VMEM_LIMIT)


def _const_spec(shape):
    zeros = (0,) * len(shape)
    return pl.BlockSpec(shape, lambda *_: zeros)


def _layer_spec(arr, layer):
    tail = tuple(arr.shape[1:])
    zeros = (0,) * len(tail)
    return pl.BlockSpec((None,) + tail, lambda *_: (layer,) + zeros)


def _rms(x, g):
    return x * lax.rsqrt(jnp.mean(x * x, axis=-1, keepdims=True) + EPS) * g


def _dot(a, b):
    return jnp.dot(a, b, preferred_element_type=F32)


def _dot_nt(a, b):
    return lax.dot_general(a, b, (((1,), (1,)), ((), ())), preferred_element_type=F32)


def _in_proj_kernel(x_ref, g_ref, w_ref, gq_ref, gkv_ref, cos_ref, sin_ref,
                    qlat_ref, ckv_ref, ckvb_ref, kr_ref, krb_ref, u_ref,
                    *, q_lora, kv_lora, ssm_w, rope):
    h = _rms(x_ref[...], g_ref[...]).astype(BF16)
    z = _dot(h, w_ref[...])
    o_kv = q_lora
    o_u = o_kv + kv_lora
    o_r = o_u + ssm_w
    qlat_ref[...] = _rms(z[:, :o_kv], gq_ref[...]).astype(BF16)
    ckv = _rms(z[:, o_kv:o_u], gkv_ref[...])
    ckv_ref[...] = ckv
    ckvb_ref[...] = ckv.astype(BF16)
    u_ref[...] = z[:, o_u:o_r]
    rot = z[:, o_r:o_r + LANE] * cos_ref[...] + z[:, o_r + LANE:o_r + 2 * LANE] * sin_ref[...]
    kr_ref[...] = rot[:, :rope]
    krb_ref[...] = rot.astype(BF16)


def _in_proj(x, g, w, gq, gkv, cos, sin, *, layer, tm, q_lora, kv_lora, ssm_w, rope):
    t, d = x.shape
    n_pos_blk = cos.shape[0] // tm
    row = lambda i: (i, 0)
    pos = lambda i: (i % n_pos_blk, 0)
    return pl.pallas_call(
        functools.partial(_in_proj_kernel, q_lora=q_lora, kv_lora=kv_lora, ssm_w=ssm_w, rope=rope),
        grid=(t // tm,),
        in_specs=[pl.BlockSpec((tm, d), row), _layer_spec(g, layer), _layer_spec(w, layer),
                  _layer_spec(gq, layer), _layer_spec(gkv, layer),
                  pl.BlockSpec((tm, LANE), pos), pl.BlockSpec((tm, LANE), pos)],
        out_specs=[pl.BlockSpec((tm, q_lora), row), pl.BlockSpec((tm, kv_lora), row),
                   pl.BlockSpec((tm, kv_lora), row), pl.BlockSpec((tm, rope), row),
                   pl.BlockSpec((tm, LANE), row), pl.BlockSpec((tm, ssm_w), row)],
        out_shape=[jax.ShapeDtypeStruct((t, q_lora), BF16), jax.ShapeDtypeStruct((t, kv_lora), F32),
                   jax.ShapeDtypeStruct((t, kv_lora), BF16), jax.ShapeDtypeStruct((t, rope), F32),
                   jax.ShapeDtypeStruct((t, LANE), BF16), jax.ShapeDtypeStruct((t, ssm_w), F32)],
        compiler_params=_params(1),
        name="in_proj",
    )(x, g, w, gq, gkv, cos, sin)


def _qkv_prompt_kernel(qlat_ref, ckvb_ref, krb_ref, wuq_ref, wuk_ref, wuv_ref, cos_ref, sin_ref,
                       q_ref, k_ref, v_ref, *, n_heads, scale):
    z = _dot(qlat_ref[...], wuq_ref[...])
    ckvb = ckvb_ref[...]
    kn = _dot(ckvb, wuk_ref[...])
    v_ref[...] = _dot(ckvb, wuv_ref[...]).astype(BF16)
    cos = cos_ref[...]
    sin = sin_ref[...]
    krb = krb_ref[...]
    o_r = n_heads * LANE
    o_s = 2 * n_heads * LANE
    for h in range(n_heads):
        lo = h * LANE
        hd = h * MXU_DIM
        q_ref[:, hd:hd + LANE] = (z[:, lo:lo + LANE] * scale).astype(BF16)
        rot = z[:, o_r + lo:o_r + lo + LANE] * cos + z[:, o_s + lo:o_s + lo + LANE] * sin
        q_ref[:, hd + LANE:hd + 2 * LANE] = (rot * scale).astype(BF16)
        k_ref[:, hd:hd + LANE] = kn[:, lo:lo + LANE].astype(BF16)
        k_ref[:, hd + LANE:hd + 2 * LANE] = krb


def _qkv_prompt(qlat, ckvb, krb, wuq, wuk, wuv, cos, sin, *, layer, tm, n_heads, scale):
    t, q_lora = qlat.shape
    kv_lora = ckvb.shape[1]
    n_pos_blk = cos.shape[0] // tm
    row = lambda i: (i, 0)
    pos = lambda i: (i % n_pos_blk, 0)
    hw = n_heads * MXU_DIM
    vw = wuv.shape[2]
    return pl.pallas_call(
        functools.partial(_qkv_prompt_kernel, n_heads=n_heads, scale=scale),
        grid=(t // tm,),
        in_specs=[pl.BlockSpec((tm, q_lora), row), pl.BlockSpec((tm, kv_lora), row),
                  pl.BlockSpec((tm, LANE), row), _layer_spec(wuq, layer), _layer_spec(wuk, layer),
                  _layer_spec(wuv, layer), pl.BlockSpec((tm, LANE), pos), pl.BlockSpec((tm, LANE), pos)],
        out_specs=[pl.BlockSpec((tm, hw), row), pl.BlockSpec((tm, hw), row), pl.BlockSpec((tm, vw), row)],
        out_shape=[jax.ShapeDtypeStruct((t, hw), BF16), jax.ShapeDtypeStruct((t, hw), BF16),
                   jax.ShapeDtypeStruct((t, vw), BF16)],
        compiler_params=_params(1),
        name="qkv_prompt",
    )(qlat, ckvb, krb, wuq, wuk, wuv, cos, sin)


def _q_sample_kernel(qlat_ref, wuq_ref, wuk_ref, cos_ref, sin_ref, q_ref, *, n_heads, kv_lora, scale):
    z = _dot(qlat_ref[...], wuq_ref[...])
    cos = cos_ref[...]
    sin = sin_ref[...]
    o_r = n_heads * LANE
    o_s = 2 * n_heads * LANE
    hw = kv_lora + LANE
    for h in range(n_heads):
        lo = h * LANE
        qn = z[:, lo:lo + LANE].astype(BF16)
        q_abs = _dot_nt(qn, wuk_ref[:, lo:lo + LANE])
        q_ref[:, h * hw:h * hw + kv_lora] = (q_abs * scale).astype(BF16)
        rot = z[:, o_r + lo:o_r + lo + LANE] * cos + z[:, o_s + lo:o_s + lo + LANE] * sin
        q_ref[:, h * hw + kv_lora:(h + 1) * hw] = (rot * scale).astype(BF16)


def _q_sample(qlat, wuq, wuk, cos, sin, *, layer, n_heads, scale):
    t = qlat.shape[0]
    kv_lora = wuk.shape[1]
    hw = kv_lora + LANE
    return pl.pallas_call(
        functools.partial(_q_sample_kernel, n_heads=n_heads, kv_lora=kv_lora, scale=scale),
        grid=(1,),
        in_specs=[_const_spec(qlat.shape), _layer_spec(wuq, layer), _layer_spec(wuk, layer),
                  _const_spec(cos.shape), _const_spec(sin.shape)],
        out_specs=_const_spec((t, n_heads * hw)),
        out_shape=jax.ShapeDtypeStruct((t, n_heads * hw), BF16),
        compiler_params=_params(1),
        name="q_sample",
    )(qlat, wuq, wuk, cos, sin)


HEADS_PER_STEP = 4


def _flash_kernel(q_ref, k_ref, v_ref, o_ref, m_ref, acc_ref, s_ref, *, blk, hps):
    i = pl.program_id(2)
    m_ref[...] = jnp.full(m_ref.shape, -jnp.inf, F32)
    acc_ref[...] = jnp.zeros(acc_ref.shape, F32)
    ones = jnp.ones((blk, LANE), BF16)
    rows = lax.broadcasted_iota(jnp.int32, (blk, blk), 0)
    cols = lax.broadcasted_iota(jnp.int32, (blk, blk), 1)
    visible = cols <= rows

    def issue_scores(slot, j):
        ks = pl.multiple_of(j * blk, blk)
        for h in range(hps):
            hc = slice(h * MXU_DIM, (h + 1) * MXU_DIM)
            s_ref[slot, h] = _dot_nt(q_ref[:, hc], k_ref[pl.ds(ks, blk), hc])

    def consume(slot, j, diagonal):
        ks = pl.multiple_of(j * blk, blk)
        for h in range(hps):
            s = s_ref[slot, h]
            if diagonal:
                s = jnp.where(visible, s, -jnp.inf)
            v1 = jnp.concatenate([v_ref[pl.ds(ks, blk), h * LANE:(h + 1) * LANE], ones], axis=-1)
            m_prev = m_ref[h]
            m_new = jnp.maximum(m_prev, jnp.max(s, axis=-1, keepdims=True))
            p = jnp.exp(s - m_new).astype(BF16)
            acc_ref[h] = jnp.exp(m_prev - m_new) * acc_ref[h] + _dot(p, v1)
            m_ref[h] = m_new

    issue_scores(0, 0)

    def block_pair(jj, carry):
        j = 2 * jj
        issue_scores(1, j + 1)
        consume(0, j, False)
        issue_scores(0, j + 2)
        consume(1, j + 1, False)
        return carry

    lax.fori_loop(0, i >> 1, block_pair, 0)

    @pl.when((i & 1) == 1)
    def _():
        issue_scores(1, i)
        consume(0, i - 1, False)
        consume(1, i, True)

    @pl.when((i & 1) == 0)
    def _():
        consume(0, i, True)

    for h in range(hps):
        acc = acc_ref[h]
        o_ref[:, h * LANE:(h + 1) * LANE] = acc[:, :LANE] / acc[:, LANE:]


def _flash(q, k, v, *, bsz, seq, n_heads, blk):
    v_dim = v.shape[1] // n_heads
    assert v_dim == LANE
    hps = HEADS_PER_STEP if n_heads % HEADS_PER_STEP == 0 else 1
    nq = seq // blk
    return pl.pallas_call(
        functools.partial(_flash_kernel, blk=blk, hps=hps),
        grid=(bsz, n_heads // hps, nq),
        in_specs=[pl.BlockSpec((blk, hps * MXU_DIM), lambda b, h, i: (b * nq + i, h)),
                  pl.BlockSpec((seq, hps * MXU_DIM), lambda b, h, i: (b, h)),
                  pl.BlockSpec((seq, hps * v_dim), lambda b, h, i: (b, h))],
        out_specs=pl.BlockSpec((blk, hps * v_dim), lambda b, h, i: (b * nq + i, h)),
        out_shape=jax.ShapeDtypeStruct((bsz * seq, n_heads * v_dim), F32),
        scratch_shapes=[pltpu.VMEM((hps, blk, 1), F32), pltpu.VMEM((hps, blk, 2 * LANE), F32),
                        pltpu.VMEM((2, hps, blk, blk), F32)],
        compiler_params=_params(3),
        name="flash_prompt",
    )(q, k, v)


def _paged_kernel(pt_ref, q_ref, cnew_ref, rnew_ref, cc_hbm, cr_hbm, o_ref,
                  cbuf, rbuf, sem_c, sem_r, *, layer, n_pages, page, kv_lora, rope):
    b = pl.program_id(0)
    slot = b & 1

    def page_copies(seq, sl, p):
        pid = pt_ref[seq, p]
        return (pltpu.make_async_copy(cc_hbm.at[layer, pid],
                                      cbuf.at[sl, pl.ds(p * page, page), :], sem_c.at[sl]),
                pltpu.make_async_copy(cr_hbm.at[layer, pid],
                                      rbuf.at[sl, :, pl.ds(p * page, page)], sem_r.at[sl]))

    def start_sequence(seq, sl):
        for p in range(n_pages):
            for cp in page_copies(seq, sl, p):
                cp.start()

    @pl.when(b == 0)
    def _():
        start_sequence(0, 0)

    @pl.when(b + 1 < pl.num_programs(0))
    def _():
        start_sequence(b + 1, 1 - slot)

    for p in range(n_pages):
        for cp in page_copies(b, slot, p):
            cp.wait()

    cb = cbuf[slot].astype(BF16)
    rb = rbuf[slot].astype(BF16)
    q = q_ref[0]
    s = _dot_nt(q[:, :kv_lora], cb) + _dot(q[:, kv_lora:kv_lora + rope], rb)
    cn = cnew_ref[0]
    qf = q.astype(F32)
    s_new = (jnp.sum(qf[:, :kv_lora] * cn, axis=-1, keepdims=True)
             + jnp.sum(qf[:, kv_lora:] * rnew_ref[0], axis=-1, keepdims=True))
    m = jnp.maximum(jnp.max(s, axis=-1, keepdims=True), s_new)
    p_old = jnp.exp(s - m)
    p_new = jnp.exp(s_new - m)
    denom = jnp.sum(p_old, axis=-1, keepdims=True) + p_new
    pb = p_old.astype(BF16)
    half = pb.shape[1] // 2
    pv = _dot(pb[:, :half], cb[:half]) + _dot(pb[:, half:], cb[half:])
    o_ref[0] = (pv + p_new * cn) / denom


def _paged(page_table, q, cnew, rnew, cache_c, cache_rt, *, layer, n_heads):
    n_seq, n_pages = page_table.shape
    _, _, page, kv_lora = cache_c.shape
    rope = cache_rt.shape[2]
    hw = q.shape[2]
    n_keys = n_pages * page
    seq3 = lambda b, pt: (b, 0, 0)
    grid_spec = pltpu.PrefetchScalarGridSpec(
        num_scalar_prefetch=1,
        grid=(n_seq,),
        in_specs=[pl.BlockSpec((1, n_heads, hw), seq3), pl.BlockSpec((1, 1, kv_lora), seq3),
                  pl.BlockSpec((1, 1, LANE), seq3),
                  pl.BlockSpec(memory_space=pl.ANY), pl.BlockSpec(memory_space=pl.ANY)],
        out_specs=pl.BlockSpec((1, n_heads, kv_lora), seq3),
        scratch_shapes=[pltpu.VMEM((2, n_keys, kv_lora), F32), pltpu.VMEM((2, rope, n_keys), F32),
                        pltpu.SemaphoreType.DMA((2,)), pltpu.SemaphoreType.DMA((2,))],
    )
    return pl.pallas_call(
        functools.partial(_paged_kernel, layer=layer, n_pages=n_pages, page=page,
                          kv_lora=kv_lora, rope=rope),
        grid_spec=grid_spec,
        out_shape=jax.ShapeDtypeStruct((n_seq, n_heads, kv_lora), F32),
        compiler_params=_params(1),
        name="paged_sample",
    )(page_table, q, cnew, rnew, cache_c, cache_rt)


def _uv_kernel(ol_ref, wuv_ref, o_ref, *, n_heads, kv_lora):
    for h in range(n_heads):
        o_ref[:, h * LANE:(h + 1) * LANE] = _dot(
            ol_ref[:, h * kv_lora:(h + 1) * kv_lora].astype(BF16), wuv_ref[:, h * LANE:(h + 1) * LANE])


def _uv_proj(o_lat, wuv, *, layer, n_heads):
    t = o_lat.shape[0]
    kv_lora, vw = wuv.shape[1:]
    return pl.pallas_call(
        functools.partial(_uv_kernel, n_heads=n_heads, kv_lora=kv_lora),
        grid=(1,),
        in_specs=[_const_spec(o_lat.shape), _layer_spec(wuv, layer)],
        out_specs=_const_spec((t, vw)),
        out_shape=jax.ShapeDtypeStruct((t, vw), F32),
        compiler_params=_params(1),
        name="uv_sample",
    )(o_lat, wuv)


def _s5_disc_kernel(lr_ref, li_ref, ldt_ref, bre_ref, bim_ref,
                    abr_ref, abi_ref, apr_ref, api_ref, bbr_ref, bbi_ref, *, n_sq):
    lr = lr_ref[...]
    li = li_ref[...]
    dt = jnp.exp(ldt_ref[...])
    mag = jnp.exp(lr * dt)
    ab_re = mag * jnp.cos(li * dt)
    ab_im = mag * jnp.sin(li * dt)
    den = lr * lr + li * li
    nr = ab_re - 1.0
    f_re = (nr * lr + ab_im * li) / den
    f_im = (ab_im * lr - nr * li) / den
    abr_ref[...] = ab_re
    abi_ref[...] = ab_im
    pr, pi = ab_re, ab_im
    for _ in range(n_sq):
        pr, pi = pr * pr - pi * pi, 2.0 * pr * pi
    apr_ref[...] = pr
    api_ref[...] = pi
    for h in range(bre_ref.shape[0]):
        b_re = bre_ref[h]
        b_im = bim_ref[h]
        bbr_ref[h] = f_re * b_re - f_im * b_im
        bbi_ref[h] = f_re * b_im + f_im * b_re


def _s5_disc(lam_re, lam_im, log_dt, b_re, b_im, *, seg_len):
    depth, g, p = lam_re.shape
    ch = b_re.shape[3]
    n_sq = int(math.log2(seg_len))
    assert 2 ** n_sq == seg_len
    b_re_t = jnp.transpose(b_re, (0, 3, 1, 2))
    b_im_t = jnp.transpose(b_im, (0, 3, 1, 2))
    gp_spec = pl.BlockSpec((None, g, p), lambda l: (l, 0, 0))
    hgp_spec = pl.BlockSpec((None, ch, g, p), lambda l: (l, 0, 0, 0))
    gp = jax.ShapeDtypeStruct((depth, g, p), F32)
    hgp = jax.ShapeDtypeStruct((depth, ch, g, p), F32)
    return pl.pallas_call(
        functools.partial(_s5_disc_kernel, n_sq=n_sq),
        grid=(depth,),
        in_specs=[gp_spec, gp_spec, pl.BlockSpec((None, g, 1), lambda l: (l, 0, 0)), hgp_spec, hgp_spec],
        out_specs=[gp_spec] * 4 + [hgp_spec] * 2,
        out_shape=[gp, gp, gp, gp, hgp, hgp],
        compiler_params=_params(1),
        name="s5_disc",
    )(lam_re, lam_im, log_dt.reshape(depth, g, 1), b_re_t, b_im_t)


def _s5_dense(bb_re, bb_im, c_re, c_im):
    depth, ch, g, p = bb_re.shape
    gk = MXU_DIM // ch
    go = LANE // ch

    def in_side(bb):
        m = jnp.transpose(bb, (0, 2, 1, 3)).reshape(depth, g // gk, gk, ch, p)
        eye = jnp.eye(gk, dtype=F32)
        return jnp.einsum('lkghp,gj->lkghjp', m, eye).reshape(depth, g // gk, gk * ch, gk * p).astype(BF16)

    def out_side(c):
        m = jnp.transpose(c, (0, 1, 3, 2)).reshape(depth, g // go, go, p, ch)
        eye = jnp.eye(go, dtype=F32)
        return jnp.einsum('lkgph,gj->lkgpjh', m, eye).reshape(depth, g // go, go * p, go * ch).astype(BF16)

    return in_side(bb_re), in_side(bb_im), out_side(c_re), out_side(-c_im)


def _s5_readout(hr, hi, u, cdr_ref, cdi_ref, d_ref, wglu_ref):
    n_ob, kc, _ = cdr_ref.shape
    hrb = hr.astype(BF16)
    hib = hi.astype(BF16)
    y = jnp.concatenate(
        [_dot(hrb[:, ob * kc:(ob + 1) * kc], cdr_ref[ob]) + _dot(hib[:, ob * kc:(ob + 1) * kc], cdi_ref[ob])
         for ob in range(n_ob)], axis=-1)
    return _s5_glu(y, u, d_ref, wglu_ref)


def _s5_glu(y, u, d_ref, wglu_ref):
    y = jax.nn.gelu(y + d_ref[...] * u).astype(BF16)
    z = _dot(y, wglu_ref[...])
    w = z.shape[1] // 2
    return z[:, :w] * jax.nn.sigmoid(z[:, w:])


def _s5_project_in(lhs, wbr_ref, wbi_ref, xr_ref, xi_ref, row0=0):
    n_kt, kc, nc = wbr_ref.shape
    rows = slice(row0, row0 + lhs.shape[0])
    for kt in range(n_kt):
        piece = lhs[:, kt * kc:(kt + 1) * kc]
        xr_ref[rows, kt * nc:(kt + 1) * nc] = _dot(piece, wbr_ref[kt])
        xi_ref[rows, kt * nc:(kt + 1) * nc] = _dot(piece, wbi_ref[kt])


def _s5_prompt_kernel(u_ref, wbr_ref, wbi_ref, ar_ref, ai_ref, apr_ref, api_ref,
                      cdr_ref, cdi_ref, d_ref, wglu_ref,
                      ssm_ref, hfr_ref, hfi_ref,
                      uperm_ref, xr_ref, xi_ref, *, sb):
    ps = pl.program_id(1)
    blk = pl.program_id(2)
    rows = sb * N_SEG
    gp = xr_ref.shape[1]
    head = slice(0, N_SEG)
    tail = slice(rows, rows + N_SEG)

    @pl.when((ps == 0) & (blk == 0))
    def _():
        xr_ref[head, :] = jnp.zeros((N_SEG, gp), F32)
        xi_ref[head, :] = jnp.zeros((N_SEG, gp), F32)

    @pl.when((ps == 1) & (blk == 0))
    def _():
        apr = apr_ref[...]
        api = api_ref[...]
        in_r = jnp.zeros(apr.shape, F32)
        in_i = jnp.zeros(apr.shape, F32)
        for seg in range(N_SEG):
            loc_r = xr_ref[seg:seg + 1, :]
            loc_i = xi_ref[seg:seg + 1, :]
            xr_ref[seg:seg + 1, :] = in_r
            xi_ref[seg:seg + 1, :] = in_i
            in_r, in_i = (loc_r + apr * in_r - api * in_i, loc_i + apr * in_i + api * in_r)

    n_kt, kc, strip = wbr_ref.shape
    n_ob, ckc, _ = cdr_ref.shape
    body = slice(N_SEG, N_SEG + rows)

    def block(readout):
        for s in range(sb):
            uperm_ref[s * N_SEG:(s + 1) * N_SEG, :] = u_ref[0, :, s, :]
        lhs = uperm_ref[...].astype(BF16)
        y = []
        for c in range(n_kt):
            cols = slice(c * strip, (c + 1) * strip)
            piece = lhs[:, c * kc:(c + 1) * kc]
            xr_ref[body, cols] = _dot(piece, wbr_ref[c])
            xi_ref[body, cols] = _dot(piece, wbi_ref[c])
            ar = ar_ref[:, cols]
            ai = ai_ref[:, cols]
            hr = xr_ref[head, cols]
            hi = xi_ref[head, cols]
            for s in range(sb):
                r = slice((s + 1) * N_SEG, (s + 2) * N_SEG)
                hr, hi = (ar * hr - ai * hi + xr_ref[r, cols], ar * hi + ai * hr + xi_ref[r, cols])
                xr_ref[r, cols] = hr
                xi_ref[r, cols] = hi
            if readout:
                for ob in range(c * strip // ckc, (c + 1) * strip // ckc):
                    oc = slice(ob * ckc, (ob + 1) * ckc)
                    y.append(_dot(xr_ref[body, oc].astype(BF16), cdr_ref[ob])
                             + _dot(xi_ref[body, oc].astype(BF16), cdi_ref[ob]))
        if readout:
            out = _s5_glu(jnp.concatenate(y, axis=-1), uperm_ref[...], d_ref, wglu_ref)
            for s in range(sb):
                ssm_ref[0, :, s, :] = out[s * N_SEG:(s + 1) * N_SEG, :]
            hfr_ref[0] = xr_ref[tail, :]
            hfi_ref[0] = xi_ref[tail, :]
        xr_ref[head, :] = xr_ref[tail, :]
        xi_ref[head, :] = xi_ref[tail, :]

    @pl.when(ps == 0)
    def _():
        block(False)

    @pl.when(ps == 1)
    def _():
        block(True)


def _s5_prompt(u, wbr, wbi, a_re, a_im, ap_re, ap_im, cdr, cdi, d_skip, wglu, *, layer, bsz, seq, sb):
    width = u.shape[1]
    gp = a_re.shape[2]
    seg_len = seq // N_SEG
    nblk = seg_len // sb
    rows = sb * N_SEG
    u4 = u.reshape(bsz, N_SEG, seg_len, width)
    blk4 = lambda b, ps, k: (b, 0, k, 0)
    weights = (wbr, wbi, a_re, a_im, ap_re, ap_im, cdr, cdi, d_skip, wglu)
    ssm, hf_r, hf_i = pl.pallas_call(
        functools.partial(_s5_prompt_kernel, sb=sb),
        grid=(bsz, 2, nblk),
        in_specs=[pl.BlockSpec((1, N_SEG, sb, width), blk4)] + [_layer_spec(w, layer) for w in weights],
        out_specs=[pl.BlockSpec((1, N_SEG, sb, width), lambda b, ps, k: (b, 0, k * ps, 0)),
                   pl.BlockSpec((1, N_SEG, gp), lambda b, ps, k: (b, 0, 0)),
                   pl.BlockSpec((1, N_SEG, gp), lambda b, ps, k: (b, 0, 0))],
        out_shape=[jax.ShapeDtypeStruct((bsz, N_SEG, seg_len, width), F32),
                   jax.ShapeDtypeStruct((bsz, N_SEG, gp), F32),
                   jax.ShapeDtypeStruct((bsz, N_SEG, gp), F32)],
        scratch_shapes=[pltpu.VMEM((rows, width), F32),
                        pltpu.VMEM((rows + N_SEG, gp), F32), pltpu.VMEM((rows + N_SEG, gp), F32)],
        compiler_params=_params(3),
        name="s5_prompt",
    )(u4, *weights)
    return ssm.reshape(bsz * seq, width), hf_r[:, N_SEG - 1], hf_i[:, N_SEG - 1]


def _s5_sample_kernel(u_ref, h0r_ref, h0i_ref, wbr_ref, wbi_ref, ar_ref, ai_ref,
                      cdr_ref, cdi_ref, d_ref, wglu_ref,
                      ssm_ref, hr_ref, hi_ref, xr_ref, xi_ref):
    u = u_ref[...]
    _s5_project_in(u.astype(BF16), wbr_ref, wbi_ref, xr_ref, xi_ref)
    ar = ar_ref[0:1, :]
    ai = ai_ref[0:1, :]
    h0r = h0r_ref[...]
    h0i = h0i_ref[...]
    hr = ar * h0r - ai * h0i + xr_ref[...]
    hi = ar * h0i + ai * h0r + xi_ref[...]
    hr_ref[...] = hr
    hi_ref[...] = hi
    ssm_ref[...] = _s5_readout(hr, hi, u, cdr_ref, cdi_ref, d_ref, wglu_ref)


def _s5_sample(u, h0_re, h0_im, wbr, wbi, a_re, a_im, cdr, cdi, d_skip, wglu, *, layer):
    t, width = u.shape
    gp = a_re.shape[2]
    st = jax.ShapeDtypeStruct((t, gp), F32)
    weights = (wbr, wbi, a_re, a_im, cdr, cdi, d_skip, wglu)
    return pl.pallas_call(
        _s5_sample_kernel,
        grid=(1,),
        in_specs=([_const_spec((t, width)), _layer_spec(h0_re, layer), _layer_spec(h0_im, layer)]
                  + [_layer_spec(w, layer) for w in weights]),
        out_specs=[_const_spec((t, width)), _const_spec((t, gp)), _const_spec((t, gp))],
        out_shape=[jax.ShapeDtypeStruct((t, width), F32), st, st],
        scratch_shapes=[pltpu.VMEM((t, gp), F32), pltpu.VMEM((t, gp), F32)],
        compiler_params=_params(1),
        name="s5_sample",
    )(u, h0_re, h0_im, *weights)


def _out_proj_kernel(x_ref, attn_ref, ssm_ref, ga_ref, gs_ref, wa_ref, ws_ref, gp_ref, o_ref):
    na = _rms(attn_ref[...], ga_ref[...]).astype(BF16)
    ns = _rms(ssm_ref[...], gs_ref[...]).astype(BF16)
    mix = _dot(na, wa_ref[...]) + _dot(ns, ws_ref[...])
    o_ref[...] = x_ref[...] + _rms(mix, gp_ref[...])


def _out_proj(x, attn, ssm, ga, gs, w_out, gp, *, layer, tm):
    t, d = x.shape
    aw = attn.shape[1]
    sw = ssm.shape[1]
    assert aw == sw and w_out.shape[1] == aw + sw
    row = lambda i: (i, 0)
    return pl.pallas_call(
        _out_proj_kernel,
        grid=(t // tm,),
        in_specs=[pl.BlockSpec((tm, d), row), pl.BlockSpec((tm, aw), row), pl.BlockSpec((tm, sw), row),
                  _layer_spec(ga, layer), _layer_spec(gs, layer),
                  pl.BlockSpec((None, aw, d), lambda i: (layer, 0, 0)),
                  pl.BlockSpec((None, sw, d), lambda i: (layer, 1, 0)),
                  _layer_spec(gp, layer)],
        out_specs=pl.BlockSpec((tm, d), row),
        out_shape=jax.ShapeDtypeStruct((t, d), F32),
        compiler_params=_params(1),
        name="out_proj",
    )(x, attn, ssm, ga, gs, w_out, w_out, gp)


def _ffn_kernel(x_ref, g1_ref, wup_ref, wdn_ref, g2_ref, o_ref, hn_ref, acc_ref):
    f = pl.program_id(1)

    @pl.when(f == 0)
    def _():
        hn_ref[...] = _rms(x_ref[...], g1_ref[...]).astype(BF16)
        acc_ref[...] = jnp.zeros(acc_ref.shape, F32)

    a = jnp.square(jnp.maximum(_dot(hn_ref[...], wup_ref[...]), 0.0)).astype(BF16)
    acc_ref[...] += _dot(a, wdn_ref[...])

    @pl.when(f == pl.num_programs(1) - 1)
    def _():
        o_ref[...] = x_ref[...] + _rms(acc_ref[...], g2_ref[...])


def _ffn(x, g1, wup, wdn, g2, *, layer, tm, tf):
    t, d = x.shape
    d_ff = wup.shape[2]
    row = lambda i, f: (i, 0)
    return pl.pallas_call(
        _ffn_kernel,
        grid=(t // tm, d_ff // tf),
        in_specs=[pl.BlockSpec((tm, d), row), _layer_spec(g1, layer),
                  pl.BlockSpec((None, d, tf), lambda i, f: (layer, 0, f)),
                  pl.BlockSpec((None, tf, d), lambda i, f: (layer, f, 0)),
                  _layer_spec(g2, layer)],
        out_specs=pl.BlockSpec((tm, d), row),
        out_shape=jax.ShapeDtypeStruct((t, d), F32),
        scratch_shapes=[pltpu.VMEM((tm, d), BF16), pltpu.VMEM((tm, d), F32)],
        compiler_params=_params(2),
        name="ffn",
    )(x, g1, wup, wdn, g2)


def _rope_tables(pos, rope):
    half = rope // 2
    inv_freq = jnp.exp(-math.log(ROPE_THETA) * 2.0 * jnp.arange(half, dtype=F32) / rope)
    ang = pos.astype(F32)[:, None] * inv_freq[None, :]
    cos, sin = jnp.cos(ang), jnp.sin(ang)
    pad = jnp.zeros((pos.shape[0], LANE - rope), F32)
    return jnp.concatenate([cos, cos, pad], axis=1), jnp.concatenate([-sin, sin, pad], axis=1)


def _swap_halves(w):
    half = w.shape[-1] // 2
    return jnp.concatenate([w[..., half:], w[..., :half]], axis=-1)


def _pad_lanes(w):
    return jnp.pad(w, [(0, 0)] * (w.ndim - 1) + [(0, LANE - w.shape[-1])])


def _tile(n, pref):
    return pref if n % pref == 0 else n


def kernel(x_prompt, x_sample, cache_kv_latent, cache_k_rope, state_ssm_re, state_ssm_im, page_table,
           g_pre_mix, w_in, g_q_lat, w_uq, g_kv_lat, w_uk, w_uv, lam_re, lam_im, log_dt, b_re, b_im,
           c_re, c_im, d_skip, w_glu, g_attn_out, g_ssm_out, w_out, g_post_mix, g_pre_mlp, w_up,
           w_down, g_post_mlp):
    bsz, seq, d = x_prompt.shape
    n_seq, dec_seq, _ = x_sample.shape
    assert dec_seq == 1
    depth = w_in.shape[0]
    q_lora = g_q_lat.shape[1]
    kv_lora, n_heads, nope = w_uk.shape[1:]
    v_dim = w_uv.shape[3]
    rope = cache_k_rope.shape[3]
    ssm_w = g_ssm_out.shape[1]
    n_grp, n_state = lam_re.shape[1:]
    gp = n_grp * n_state
    page = cache_kv_latent.shape[2]
    past_len = page_table.shape[1] * page
    assert nope == LANE and v_dim == LANE and rope * 2 == LANE
    scale = (nope + rope) ** -0.5
    seg_len = seq // N_SEG

    cos_p, sin_p = _rope_tables(jnp.arange(seq, dtype=jnp.int32), rope)
    cos_s, sin_s = _rope_tables(jnp.full((n_seq,), past_len, jnp.int32), rope)
    cache_rt = jnp.swapaxes(cache_k_rope, 2, 3)

    o_r, o_u = q_lora + kv_lora, q_lora + kv_lora + rope
    w_kr = w_in[:, :, o_r:o_u]
    w_in_b = jnp.concatenate(
        [w_in[:, :, :o_r], w_in[:, :, o_u:], _pad_lanes(w_kr), _pad_lanes(_swap_halves(w_kr))],
        axis=2).astype(BF16)
    wq = w_uq.reshape(depth, q_lora, n_heads, nope + rope)
    wq_r = wq[..., nope:]
    w_uq_b = jnp.concatenate(
        [wq[..., :nope].reshape(depth, q_lora, -1), _pad_lanes(wq_r).reshape(depth, q_lora, -1),
         _pad_lanes(_swap_halves(wq_r)).reshape(depth, q_lora, -1)], axis=2).astype(BF16)
    w_uk_b = w_uk.reshape(depth, kv_lora, n_heads * nope).astype(BF16)
    w_uv_b = w_uv.reshape(depth, kv_lora, n_heads * v_dim).astype(BF16)
    w_glu_b = w_glu.astype(BF16)
    w_out_b = w_out.astype(BF16)
    w_up_b = w_up.astype(BF16)
    w_dn_b = w_down.astype(BF16)
    gain = lambda a: a.reshape(depth, 1, -1)
    g_pre_mix, g_q_lat, g_kv_lat, g_attn_out, g_ssm_out, g_post_mix, g_pre_mlp, g_post_mlp = map(
        gain, (g_pre_mix, g_q_lat, g_kv_lat, g_attn_out, g_ssm_out, g_post_mix, g_pre_mlp, g_post_mlp))
    d_all = gain(d_skip)

    ab_re, ab_im, ap_re, ap_im, bb_re, bb_im = _s5_disc(lam_re, lam_im, log_dt, b_re, b_im, seg_len=seg_len)
    wbr, wbi, cdr, cdi = _s5_dense(bb_re, bb_im, c_re, c_im)
    a_re = jnp.broadcast_to(ab_re.reshape(depth, 1, gp), (depth, N_SEG, gp))
    a_im = jnp.broadcast_to(ab_im.reshape(depth, 1, gp), (depth, N_SEG, gp))
    ap_re = ap_re.reshape(depth, 1, gp)
    ap_im = ap_im.reshape(depth, 1, gp)
    h0_re = state_ssm_re.reshape(depth, n_seq, gp)
    h0_im = state_ssm_im.reshape(depth, n_seq, gp)

    tm_p = _tile(seq, 512)
    tf = _tile(w_up.shape[2], 1024)
    xp = x_prompt.reshape(bsz * seq, d)
    xs = x_sample.reshape(n_seq, d)
    in_kw = dict(q_lora=q_lora, kv_lora=kv_lora, ssm_w=ssm_w, rope=rope)

    outs = [[] for _ in range(8)]
    for l in range(depth):
        qlat, ckv, ckvb, kr, krb, u = _in_proj(
            xp, g_pre_mix, w_in_b, g_q_lat, g_kv_lat, cos_p, sin_p, layer=l, tm=tm_p, **in_kw)
        q, k, v = _qkv_prompt(qlat, ckvb, krb, w_uq_b, w_uk_b, w_uv_b, cos_p, sin_p,
                              layer=l, tm=tm_p, n_heads=n_heads, scale=scale)
        attn = _flash(q, k, v, bsz=bsz, seq=seq, n_heads=n_heads, blk=_tile(seq, 512))
        ssm, hf_r, hf_i = _s5_prompt(u, wbr, wbi, a_re, a_im, ap_re, ap_im, cdr, cdi, d_all, w_glu_b,
                                     layer=l, bsz=bsz, seq=seq, sb=_tile(seg_len, 32))
        x1 = _out_proj(xp, attn, ssm, g_attn_out, g_ssm_out, w_out_b, g_post_mix, layer=l, tm=tm_p)
        xp = _ffn(x1, g_pre_mlp, w_up_b, w_dn_b, g_post_mlp, layer=l, tm=tm_p, tf=tf)
        outs[0].append(ckv.reshape(bsz, seq, kv_lora))
        outs[1].append(kr.reshape(bsz, seq, rope))
        outs[2].append(hf_r.reshape(bsz, n_grp, n_state))
        outs[3].append(hf_i.reshape(bsz, n_grp, n_state))

        qlat, ckv, ckvb, kr, krb, u = _in_proj(
            xs, g_pre_mix, w_in_b, g_q_lat, g_kv_lat, cos_s, sin_s, layer=l, tm=n_seq, **in_kw)
        qs = _q_sample(qlat, w_uq_b, w_uk_b, cos_s, sin_s, layer=l, n_heads=n_heads, scale=scale)
        o_lat = _paged(page_table, qs.reshape(n_seq, n_heads, kv_lora + LANE),
                       ckv.reshape(n_seq, 1, kv_lora), _pad_lanes(kr).reshape(n_seq, 1, LANE),
                       cache_kv_latent, cache_rt, layer=l, n_heads=n_heads)
        attn = _uv_proj(o_lat.reshape(n_seq, n_heads * kv_lora), w_uv_b, layer=l, n_heads=n_heads)
        ssm, h_r, h_i = _s5_sample(u, h0_re, h0_im, wbr, wbi, a_re, a_im, cdr, cdi, d_all, w_glu_b, layer=l)
        x1 = _out_proj(xs, attn, ssm, g_attn_out, g_ssm_out, w_out_b, g_post_mix, layer=l, tm=n_seq)
        xs = _ffn(x1, g_pre_mlp, w_up_b, w_dn_b, g_post_mlp, layer=l, tm=n_seq, tf=tf)
        outs[4].append(ckv.reshape(n_seq, 1, kv_lora))
        outs[5].append(kr.reshape(n_seq, 1, rope))
        outs[6].append(h_r.reshape(n_seq, n_grp, n_state))
        outs[7].append(h_i.reshape(n_seq, n_grp, n_state))

    return (xp.reshape(bsz, seq, d), xs.reshape(n_seq, 1, d)) + tuple(jnp.stack(o) for o in outs)
```

```python
import functools
import math

import jax
import jax.numpy as jnp
from jax import lax
from jax.experimental import pallas as pl
from jax.experimental.pallas import tpu as pltpu

F32 = jnp.float32
BF16 = jnp.bfloat16

EPS = 1e-6
ROPE_THETA = 10000.0
LANE = 128
SUBLANE = 8
MXU_DIM = 256
VMEM_LIMIT = 56 * 1024 * 1024
N_SEG = SUBLANE


def _params(n_axes):
    return pltpu.CompilerParams(dimension_semantics=("arbitrary",) * n_axes,
                                vmem_limit_bytes=VMEM_LIMIT)


def _const_spec(shape):
    zeros = (0,) * len(shape)
    return pl.BlockSpec(shape, lambda *_: zeros)


def _layer_spec(arr, layer):
    tail = tuple(arr.shape[1:])
    zeros = (0,) * len(tail)
    return pl.BlockSpec((None,) + tail, lambda *_: (layer,) + zeros)


def _rms(x, g):
    return x * lax.rsqrt(jnp.mean(x * x, axis=-1, keepdims=True) + EPS) * g


def _dot(a, b):
    return jnp.dot(a, b, preferred_element_type=F32)


def _dot_nt(a, b):
    return lax.dot_general(a, b, (((1,), (1,)), ((), ())), preferred_element_type=F32)


def _in_proj_kernel(x_ref, g_ref, w_ref, gq_ref, gkv_ref, cos_ref, sin_ref,
                    qlat_ref, ckv_ref, ckvb_ref, kr_ref, krb_ref, u_ref,
                    *, q_lora, kv_lora, ssm_w, rope):
    h = _rms(x_ref[...], g_ref[...]).astype(BF16)
    z = _dot(h, w_ref[...])
    o_kv = q_lora
    o_u = o_kv + kv_lora
    o_r = o_u + ssm_w
    qlat_ref[...] = _rms(z[:, :o_kv], gq_ref[...]).astype(BF16)
    ckv = _rms(z[:, o_kv:o_u], gkv_ref[...])
    ckv_ref[...] = ckv
    ckvb_ref[...] = ckv.astype(BF16)
    u_ref[...] = z[:, o_u:o_r]
    rot = z[:, o_r:o_r + LANE] * cos_ref[...] + z[:, o_r + LANE:o_r + 2 * LANE] * sin_ref[...]
    kr_ref[...] = rot[:, :rope]
    krb_ref[...] = rot.astype(BF16)


def _in_proj(x, g, w, gq, gkv, cos, sin, *, layer, tm, q_lora, kv_lora, ssm_w, rope):
    t, d = x.shape
    n_pos_blk = cos.shape[0] // tm
    row = lambda i: (i, 0)
    pos = lambda i: (i % n_pos_blk, 0)
    return pl.pallas_call(
        functools.partial(_in_proj_kernel, q_lora=q_lora, kv_lora=kv_lora, ssm_w=ssm_w, rope=rope),
        grid=(t // tm,),
        in_specs=[pl.BlockSpec((tm, d), row), _layer_spec(g, layer), _layer_spec(w, layer),
                  _layer_spec(gq, layer), _layer_spec(gkv, layer),
                  pl.BlockSpec((tm, LANE), pos), pl.BlockSpec((tm, LANE), pos)],
        out_specs=[pl.BlockSpec((tm, q_lora), row), pl.BlockSpec((tm, kv_lora), row),
                   pl.BlockSpec((tm, kv_lora), row), pl.BlockSpec((tm, rope), row),
                   pl.BlockSpec((tm, LANE), row), pl.BlockSpec((tm, ssm_w), row)],
        out_shape=[jax.ShapeDtypeStruct((t, q_lora), BF16), jax.ShapeDtypeStruct((t, kv_lora), F32),
                   jax.ShapeDtypeStruct((t, kv_lora), BF16), jax.ShapeDtypeStruct((t, rope), F32),
                   jax.ShapeDtypeStruct((t, LANE), BF16), jax.ShapeDtypeStruct((t, ssm_w), F32)],
        compiler_params=_params(1),
        name="in_proj",
    )(x, g, w, gq, gkv, cos, sin)


def _qkv_prompt_kernel(qlat_ref, ckvb_ref, krb_ref, wuq_ref, wuk_ref, wuv_ref, cos_ref, sin_ref,
                       q_ref, k_ref, v_ref, *, n_heads, scale):
    z = _dot(qlat_ref[...], wuq_ref[...])
    ckvb = ckvb_ref[...]
    kn = _dot(ckvb, wuk_ref[...])
    v_ref[...] = _dot(ckvb, wuv_ref[...]).astype(BF16)
    cos = cos_ref[...]
    sin = sin_ref[...]
    krb = krb_ref[...]
    o_r = n_heads * LANE
    o_s = 2 * n_heads * LANE
    for h in range(n_heads):
        lo = h * LANE
        hd = h * MXU_DIM
        q_ref[:, hd:hd + LANE] = (z[:, lo:lo + LANE] * scale).astype(BF16)
        rot = z[:, o_r + lo:o_r + lo + LANE] * cos + z[:, o_s + lo:o_s + lo + LANE] * sin
        q_ref[:, hd + LANE:hd + 2 * LANE] = (rot * scale).astype(BF16)
        k_ref[:, hd:hd + LANE] = kn[:, lo:lo + LANE].astype(BF16)
        k_ref[:, hd + LANE:hd + 2 * LANE] = krb


def _qkv_prompt(qlat, ckvb, krb, wuq, wuk, wuv, cos, sin, *, layer, tm, n_heads, scale):
    t, q_lora = qlat.shape
    kv_lora = ckvb.shape[1]
    n_pos_blk = cos.shape[0] // tm
    row = lambda i: (i, 0)
    pos = lambda i: (i % n_pos_blk, 0)
    hw = n_heads * MXU_DIM
    vw = wuv.shape[2]
    return pl.pallas_call(
        functools.partial(_qkv_prompt_kernel, n_heads=n_heads, scale=scale),
        grid=(t // tm,),
        in_specs=[pl.BlockSpec((tm, q_lora), row), pl.BlockSpec((tm, kv_lora), row),
                  pl.BlockSpec((tm, LANE), row), _layer_spec(wuq, layer), _layer_spec(wuk, layer),
                  _layer_spec(wuv, layer), pl.BlockSpec((tm, LANE), pos), pl.BlockSpec((tm, LANE), pos)],
        out_specs=[pl.BlockSpec((tm, hw), row), pl.BlockSpec((tm, hw), row), pl.BlockSpec((tm, vw), row)],
        out_shape=[jax.ShapeDtypeStruct((t, hw), BF16), jax.ShapeDtypeStruct((t, hw), BF16),
                   jax.ShapeDtypeStruct((t, vw), BF16)],
        compiler_params=_params(1),
        name="qkv_prompt",
    )(qlat, ckvb, krb, wuq, wuk, wuv, cos, sin)


def _q_sample_kernel(qlat_ref, wuq_ref, wuk_ref, cos_ref, sin_ref, q_ref, *, n_heads, kv_lora, scale):
    z = _dot(qlat_ref[...], wuq_ref[...])
    cos = cos_ref[...]
    sin = sin_ref[...]
    o_r = n_heads * LANE
    o_s = 2 * n_heads * LANE
    hw = kv_lora + LANE
    for h in range(n_heads):
        lo = h * LANE
        qn = z[:, lo:lo + LANE].astype(BF16)
        q_abs = _dot_nt(qn, wuk_ref[:, lo:lo + LANE])
        q_ref[:, h * hw:h * hw + kv_lora] = (q_abs * scale).astype(BF16)
        rot = z[:, o_r + lo:o_r + lo + LANE] * cos + z[:, o_s + lo:o_s + lo + LANE] * sin
        q_ref[:, h * hw + kv_lora:(h + 1) * hw] = (rot * scale).astype(BF16)


def _q_sample(qlat, wuq, wuk, cos, sin, *, layer, n_heads, scale):
    t = qlat.shape[0]
    kv_lora = wuk.shape[1]
    hw = kv_lora + LANE
    return pl.pallas_call(
        functools.partial(_q_sample_kernel, n_heads=n_heads, kv_lora=kv_lora, scale=scale),
        grid=(1,),
        in_specs=[_const_spec(qlat.shape), _layer_spec(wuq, layer), _layer_spec(wuk, layer),
                  _const_spec(cos.shape), _const_spec(sin.shape)],
        out_specs=_const_spec((t, n_heads * hw)),
        out_shape=jax.ShapeDtypeStruct((t, n_heads * hw), BF16),
        compiler_params=_params(1),
        name="q_sample",
    )(qlat, wuq, wuk, cos, sin)


HEADS_PER_STEP = 4


def _flash_kernel(q_ref, k_ref, v_ref, o_ref, m_ref, acc_ref, s_ref, *, blk, hps):
    i = pl.program_id(2)
    m_ref[...] = jnp.full(m_ref.shape, -jnp.inf, F32)
    acc_ref[...] = jnp.zeros(acc_ref.shape, F32)
    ones = jnp.ones((blk, LANE), BF16)
    rows = lax.broadcasted_iota(jnp.int32, (blk, blk), 0)
    cols = lax.broadcasted_iota(jnp.int32, (blk, blk), 1)
    visible = cols <= rows

    def issue_scores(slot, j):
        ks = pl.multiple_of(j * blk, blk)
        for h in range(hps):
            hc = slice(h * MXU_DIM, (h + 1) * MXU_DIM)
            s_ref[slot, h] = _dot_nt(q_ref[:, hc], k_ref[pl.ds(ks, blk), hc])

    def consume(slot, j, diagonal):
        ks = pl.multiple_of(j * blk, blk)
        for h in range(hps):
            s = s_ref[slot, h]
            if diagonal:
                s = jnp.where(visible, s, -jnp.inf)
            v1 = jnp.concatenate([v_ref[pl.ds(ks, blk), h * LANE:(h + 1) * LANE], ones], axis=-1)
            m_prev = m_ref[h]
            m_new = jnp.maximum(m_prev, jnp.max(s, axis=-1, keepdims=True))
            p = jnp.exp(s - m_new).astype(BF16)
            acc_ref[h] = jnp.exp(m_prev - m_new) * acc_ref[h] + _dot(p, v1)
            m_ref[h] = m_new

    issue_scores(0, 0)

    def block_pair(jj, carry):
        j = 2 * jj
        issue_scores(1, j + 1)
        consume(0, j, False)
        issue_scores(0, j + 2)
        consume(1, j + 1, False)
        return carry

    lax.fori_loop(0, i >> 1, block_pair, 0)

    @pl.when((i & 1) == 1)
    def _():
        issue_scores(1, i)
        consume(0, i - 1, False)
        consume(1, i, True)

    @pl.when((i & 1) == 0)
    def _():
        consume(0, i, True)

    for h in range(hps):
        acc = acc_ref[h]
        o_ref[:, h * LANE:(h + 1) * LANE] = acc[:, :LANE] / acc[:, LANE:]


def _flash(q, k, v, *, bsz, seq, n_heads, blk):
    v_dim = v.shape[1] // n_heads
    assert v_dim == LANE
    hps = HEADS_PER_STEP if n_heads % HEADS_PER_STEP == 0 else 1
    nq = seq // blk
    return pl.pallas_call(
        functools.partial(_flash_kernel, blk=blk, hps=hps),
        grid=(bsz, n_heads // hps, nq),
        in_specs=[pl.BlockSpec((blk, hps * MXU_DIM), lambda b, h, i: (b * nq + i, h)),
                  pl.BlockSpec((seq, hps * MXU_DIM), lambda b, h, i: (b, h)),
                  pl.BlockSpec((seq, hps * v_dim), lambda b, h, i: (b, h))],
        out_specs=pl.BlockSpec((blk, hps * v_dim), lambda b, h, i: (b * nq + i, h)),
        out_shape=jax.ShapeDtypeStruct((bsz * seq, n_heads * v_dim), F32),
        scratch_shapes=[pltpu.VMEM((hps, blk, 1), F32), pltpu.VMEM((hps, blk, 2 * LANE), F32),
                        pltpu.VMEM((2, hps, blk, blk), F32)],
        compiler_params=_params(3),
        name="flash_prompt",
    )(q, k, v)


PAGED_SLOTS = 3
def _paged_kernel(pt_ref, q_ref, cnew_ref, rnew_ref, cc_hbm, cr_hbm, o_ref,
                  cbuf, rbuf, sem_c, sem_r, *, layer, n_pages, page, kv_lora, rope):
    b = pl.program_id(0)
    n_seq = pl.num_programs(0)
    slot = b % PAGED_SLOTS

    def page_copies(seq, sl, p):
        pid = pt_ref[seq, p]
        return (pltpu.make_async_copy(cc_hbm.at[layer, pid],
                                      cbuf.at[sl, pl.ds(p * page, page), :], sem_c.at[sl]),
                pltpu.make_async_copy(cr_hbm.at[layer, pid],
                                      rbuf.at[sl, :, pl.ds(p * page, page)], sem_r.at[sl]))

    def start_sequence(seq, sl):
        for p in range(n_pages):
            for cp in page_copies(seq, sl, p):
                cp.start()

    @pl.when(b == 0)
    def _():
        for ahead in range(PAGED_SLOTS - 1):
            start_sequence(ahead, ahead)

    @pl.when(b + PAGED_SLOTS - 1 < n_seq)
    def _():
        start_sequence(b + PAGED_SLOTS - 1, (b + PAGED_SLOTS - 1) % PAGED_SLOTS)

    for p in range(n_pages):
        for cp in page_copies(b, slot, p):
            cp.wait()

    cb = cbuf[slot].astype(BF16)
    rb = rbuf[slot].astype(BF16)
    q = q_ref[0]
    s = _dot_nt(q[:, :kv_lora], cb) + _dot(q[:, kv_lora:kv_lora + rope], rb)
    cn = cnew_ref[0]
    qf = q.astype(F32)
    s_new = (jnp.sum(qf[:, :kv_lora] * cn, axis=-1, keepdims=True)
             + jnp.sum(qf[:, kv_lora:] * rnew_ref[0], axis=-1, keepdims=True))
    m = jnp.maximum(jnp.max(s, axis=-1, keepdims=True), s_new)
    p_old = jnp.exp(s - m)
    p_new = jnp.exp(s_new - m)
    denom = jnp.sum(p_old, axis=-1, keepdims=True) + p_new
    pb = p_old.astype(BF16)
    half = pb.shape[1] // 2
    pv = _dot(pb[:, :half], cb[:half]) + _dot(pb[:, half:], cb[half:])
    o_ref[0] = (pv + p_new * cn) / denom


def _paged(page_table, q, cnew, rnew, cache_c, cache_rt, *, layer, n_heads):
    n_seq, n_pages = page_table.shape
    _, _, page, kv_lora = cache_c.shape
    rope = cache_rt.shape[2]
    hw = q.shape[2]
    n_keys = n_pages * page
    assert n_seq >= PAGED_SLOTS - 1
    seq3 = lambda b, pt: (b, 0, 0)
    grid_spec = pltpu.PrefetchScalarGridSpec(
        num_scalar_prefetch=1,
        grid=(n_seq,),
        in_specs=[pl.BlockSpec((1, n_heads, hw), seq3), pl.BlockSpec((1, 1, kv_lora), seq3),
                  pl.BlockSpec((1, 1, LANE), seq3),
                  pl.BlockSpec(memory_space=pl.ANY), pl.BlockSpec(memory_space=pl.ANY)],
        out_specs=pl.BlockSpec((1, n_heads, kv_lora), seq3),
        scratch_shapes=[pltpu.VMEM((PAGED_SLOTS, n_keys, kv_lora), F32),
                        pltpu.VMEM((PAGED_SLOTS, rope, n_keys), F32),
                        pltpu.SemaphoreType.DMA((PAGED_SLOTS,)), pltpu.SemaphoreType.DMA((PAGED_SLOTS,))],
    )
    return pl.pallas_call(
        functools.partial(_paged_kernel, layer=layer, n_pages=n_pages, page=page,
                          kv_lora=kv_lora, rope=rope),
        grid_spec=grid_spec,
        out_shape=jax.ShapeDtypeStruct((n_seq, n_heads, kv_lora), F32),
        compiler_params=_params(1),
        name="paged_sample",
    )(page_table, q, cnew, rnew, cache_c, cache_rt)


def _uv_kernel(ol_ref, wuv_ref, o_ref, *, n_heads, kv_lora):
    for h in range(n_heads):
        o_ref[:, h * LANE:(h + 1) * LANE] = _dot(
            ol_ref[:, h * kv_lora:(h + 1) * kv_lora].astype(BF16), wuv_ref[:, h * LANE:(h + 1) * LANE])


def _uv_proj(o_lat, wuv, *, layer, n_heads):
    t = o_lat.shape[0]
    kv_lora, vw = wuv.shape[1:]
    return pl.pallas_call(
        functools.partial(_uv_kernel, n_heads=n_heads, kv_lora=kv_lora),
        grid=(1,),
        in_specs=[_const_spec(o_lat.shape), _layer_spec(wuv, layer)],
        out_specs=_const_spec((t, vw)),
        out_shape=jax.ShapeDtypeStruct((t, vw), F32),
        compiler_params=_params(1),
        name="uv_sample",
    )(o_lat, wuv)


def _s5_disc_kernel(lr_ref, li_ref, ldt_ref, bre_ref, bim_ref,
                    abr_ref, abi_ref, apr_ref, api_ref, bbr_ref, bbi_ref, *, n_sq):
    lr = lr_ref[...]
    li = li_ref[...]
    dt = jnp.exp(ldt_ref[...])
    mag = jnp.exp(lr * dt)
    ab_re = mag * jnp.cos(li * dt)
    ab_im = mag * jnp.sin(li * dt)
    den = lr * lr + li * li
    nr = ab_re - 1.0
    f_re = (nr * lr + ab_im * li) / den
    f_im = (ab_im * lr - nr * li) / den
    abr_ref[...] = ab_re
    abi_ref[...] = ab_im
    pr, pi = ab_re, ab_im
    for _ in range(n_sq):
        pr, pi = pr * pr - pi * pi, 2.0 * pr * pi
    apr_ref[...] = pr
    api_ref[...] = pi
    for h in range(bre_ref.shape[0]):
        b_re = bre_ref[h]
        b_im = bim_ref[h]
        bbr_ref[h] = f_re * b_re - f_im * b_im
        bbi_ref[h] = f_re * b_im + f_im * b_re


def _s5_disc(lam_re, lam_im, log_dt, b_re, b_im, *, seg_len):
    depth, g, p = lam_re.shape
    ch = b_re.shape[3]
    n_sq = int(math.log2(seg_len))
    assert 2 ** n_sq == seg_len
    b_re_t = jnp.transpose(b_re, (0, 3, 1, 2))
    b_im_t = jnp.transpose(b_im, (0, 3, 1, 2))
    gp_spec = pl.BlockSpec((None, g, p), lambda l: (l, 0, 0))
    hgp_spec = pl.BlockSpec((None, ch, g, p), lambda l: (l, 0, 0, 0))
    gp = jax.ShapeDtypeStruct((depth, g, p), F32)
    hgp = jax.ShapeDtypeStruct((depth, ch, g, p), F32)
    return pl.pallas_call(
        functools.partial(_s5_disc_kernel, n_sq=n_sq),
        grid=(depth,),
        in_specs=[gp_spec, gp_spec, pl.BlockSpec((None, g, 1), lambda l: (l, 0, 0)), hgp_spec, hgp_spec],
        out_specs=[gp_spec] * 4 + [hgp_spec] * 2,
        out_shape=[gp, gp, gp, gp, hgp, hgp],
        compiler_params=_params(1),
        name="s5_disc",
    )(lam_re, lam_im, log_dt.reshape(depth, g, 1), b_re_t, b_im_t)


def _s5_dense(bb_re, bb_im, c_re, c_im):
    depth, ch, g, p = bb_re.shape
    gk = MXU_DIM // ch
    go = LANE // ch

    def in_side(bb):
        m = jnp.transpose(bb, (0, 2, 1, 3)).reshape(depth, g // gk, gk, ch, p)
        eye = jnp.eye(gk, dtype=F32)
        return jnp.einsum('lkghp,gj->lkghjp', m, eye).reshape(depth, g // gk, gk * ch, gk * p).astype(BF16)

    def out_side(c):
        m = jnp.transpose(c, (0, 1, 3, 2)).reshape(depth, g // go, go, p, ch)
        eye = jnp.eye(go, dtype=F32)
        return jnp.einsum('lkgph,gj->lkgpjh', m, eye).reshape(depth, g // go, go * p, go * ch).astype(BF16)

    return in_side(bb_re), in_side(bb_im), out_side(c_re), out_side(-c_im)


def _s5_readout(hr, hi, u, cdr_ref, cdi_ref, d_ref, wglu_ref):
    n_ob, kc, _ = cdr_ref.shape
    hrb = hr.astype(BF16)
    hib = hi.astype(BF16)
    y = jnp.concatenate(
        [_dot(hrb[:, ob * kc:(ob + 1) * kc], cdr_ref[ob]) + _dot(hib[:, ob * kc:(ob + 1) * kc], cdi_ref[ob])
         for ob in range(n_ob)], axis=-1)
    return _s5_glu(y, u, d_ref, wglu_ref)


def _s5_glu(y, u, d_ref, wglu_ref):
    y = jax.nn.gelu(y + d_ref[...] * u).astype(BF16)
    z = _dot(y, wglu_ref[...])
    w = z.shape[1] // 2
    return z[:, :w] * jax.nn.sigmoid(z[:, w:])


def _s5_project_in(lhs, wbr_ref, wbi_ref, xr_ref, xi_ref, row0=0):
    n_kt, kc, nc = wbr_ref.shape
    rows = slice(row0, row0 + lhs.shape[0])
    for kt in range(n_kt):
        piece = lhs[:, kt * kc:(kt + 1) * kc]
        xr_ref[rows, kt * nc:(kt + 1) * nc] = _dot(piece, wbr_ref[kt])
        xi_ref[rows, kt * nc:(kt + 1) * nc] = _dot(piece, wbi_ref[kt])


def _s5_prompt_kernel(u_ref, wbr_ref, wbi_ref, ar_ref, ai_ref, apr_ref, api_ref,
                      cdr_ref, cdi_ref, d_ref, wglu_ref,
                      ssm_ref, hfr_ref, hfi_ref,
                      uperm_ref, xr_ref, xi_ref, *, sb):
    ps = pl.program_id(1)
    blk = pl.program_id(2)
    rows = sb * N_SEG
    gp = xr_ref.shape[1]
    head = slice(0, N_SEG)
    tail = slice(rows, rows + N_SEG)

    @pl.when((ps == 0) & (blk == 0))
    def _():
        xr_ref[head, :] = jnp.zeros((N_SEG, gp), F32)
        xi_ref[head, :] = jnp.zeros((N_SEG, gp), F32)

    @pl.when((ps == 1) & (blk == 0))
    def _():
        apr = apr_ref[...]
        api = api_ref[...]
        in_r = jnp.zeros(apr.shape, F32)
        in_i = jnp.zeros(apr.shape, F32)
        for seg in range(N_SEG):
            loc_r = xr_ref[seg:seg + 1, :]
            loc_i = xi_ref[seg:seg + 1, :]
            xr_ref[seg:seg + 1, :] = in_r
            xi_ref[seg:seg + 1, :] = in_i
            in_r, in_i = (loc_r + apr * in_r - api * in_i, loc_i + apr * in_i + api * in_r)

    n_kt, kc, strip = wbr_ref.shape
    n_ob, ckc, _ = cdr_ref.shape
    body = slice(N_SEG, N_SEG + rows)

    def block(readout):
        for s in range(sb):
            uperm_ref[s * N_SEG:(s + 1) * N_SEG, :] = u_ref[0, :, s, :]
        lhs = uperm_ref[...].astype(BF16)
        y = []
        for c in range(n_kt):
            cols = slice(c * strip, (c + 1) * strip)
            piece = lhs[:, c * kc:(c + 1) * kc]
            xr_ref[body, cols] = _dot(piece, wbr_ref[c])
            xi_ref[body, cols] = _dot(piece, wbi_ref[c])
            ar = ar_ref[:, cols]
            ai = ai_ref[:, cols]
            hr = xr_ref[head, cols]
            hi = xi_ref[head, cols]
            for s in range(sb):
                r = slice((s + 1) * N_SEG, (s + 2) * N_SEG)
                hr, hi = (ar * hr - ai * hi + xr_ref[r, cols], ar * hi + ai * hr + xi_ref[r, cols])
                xr_ref[r, cols] = hr
                xi_ref[r, cols] = hi
            if readout:
                for ob in range(c * strip // ckc, (c + 1) * strip // ckc):
                    oc = slice(ob * ckc, (ob + 1) * ckc)
                    y.append(_dot(xr_ref[body, oc].astype(BF16), cdr_ref[ob])
                             + _dot(xi_ref[body, oc].astype(BF16), cdi_ref[ob]))
        if readout:
            out = _s5_glu(jnp.concatenate(y, axis=-1), uperm_ref[...], d_ref, wglu_ref)
            for s in range(sb):
                ssm_ref[0, :, s, :] = out[s * N_SEG:(s + 1) * N_SEG, :]
            hfr_ref[0] = xr_ref[tail, :]
            hfi_ref[0] = xi_ref[tail, :]
        xr_ref[head, :] = xr_ref[tail, :]
        xi_ref[head, :] = xi_ref[tail, :]

    @pl.when(ps == 0)
    def _():
        block(False)

    @pl.when(ps == 1)
    def _():
        block(True)


def _s5_prompt(u, wbr, wbi, a_re, a_im, ap_re, ap_im, cdr, cdi, d_skip, wglu, *, layer, bsz, seq, sb):
    width = u.shape[1]
    gp = a_re.shape[2]
    seg_len = seq // N_SEG
    nblk = seg_len // sb
    rows = sb * N_SEG
    u4 = u.reshape(bsz, N_SEG, seg_len, width)
    blk4 = lambda b, ps, k: (b, 0, k, 0)
    weights = (wbr, wbi, a_re, a_im, ap_re, ap_im, cdr, cdi, d_skip, wglu)
    ssm, hf_r, hf_i = pl.pallas_call(
        functools.partial(_s5_prompt_kernel, sb=sb),
        grid=(bsz, 2, nblk),
        in_specs=[pl.BlockSpec((1, N_SEG, sb, width), blk4)] + [_layer_spec(w, layer) for w in weights],
        out_specs=[pl.BlockSpec((1, N_SEG, sb, width), lambda b, ps, k: (b, 0, k * ps, 0)),
                   pl.BlockSpec((1, N_SEG, gp), lambda b, ps, k: (b, 0, 0)),
                   pl.BlockSpec((1, N_SEG, gp), lambda b, ps, k: (b, 0, 0))],
        out_shape=[jax.ShapeDtypeStruct((bsz, N_SEG, seg_len, width), F32),
                   jax.ShapeDtypeStruct((bsz, N_SEG, gp), F32),
                   jax.ShapeDtypeStruct((bsz, N_SEG, gp), F32)],
        scratch_shapes=[pltpu.VMEM((rows, width), F32),
                        pltpu.VMEM((rows + N_SEG, gp), F32), pltpu.VMEM((rows + N_SEG, gp), F32)],
        compiler_params=_params(3),
        name="s5_prompt",
    )(u4, *weights)
    return ssm.reshape(bsz * seq, width), hf_r[:, N_SEG - 1], hf_i[:, N_SEG - 1]


def _s5_sample_kernel(u_ref, h0r_ref, h0i_ref, wbr_ref, wbi_ref, ar_ref, ai_ref,
                      cdr_ref, cdi_ref, d_ref, wglu_ref,
                      ssm_ref, hr_ref, hi_ref, xr_ref, xi_ref):
    u = u_ref[...]
    _s5_project_in(u.astype(BF16), wbr_ref, wbi_ref, xr_ref, xi_ref)
    ar = ar_ref[0:1, :]
    ai = ai_ref[0:1, :]
    h0r = h0r_ref[...]
    h0i = h0i_ref[...]
    hr = ar * h0r - ai * h0i + xr_ref[...]
    hi = ar * h0i + ai * h0r + xi_ref[...]
    hr_ref[...] = hr
    hi_ref[...] = hi
    ssm_ref[...] = _s5_readout(hr, hi, u, cdr_ref, cdi_ref, d_ref, wglu_ref)


def _s5_sample(u, h0_re, h0_im, wbr, wbi, a_re, a_im, cdr, cdi, d_skip, wglu, *, layer):
    t, width = u.shape
    gp = a_re.shape[2]
    st = jax.ShapeDtypeStruct((t, gp), F32)
    weights = (wbr, wbi, a_re, a_im, cdr, cdi, d_skip, wglu)
    return pl.pallas_call(
        _s5_sample_kernel,
        grid=(1,),
        in_specs=([_const_spec((t, width)), _layer_spec(h0_re, layer), _layer_spec(h0_im, layer)]
                  + [_layer_spec(w, layer) for w in weights]),
        out_specs=[_const_spec((t, width)), _const_spec((t, gp)), _const_spec((t, gp))],
        out_shape=[jax.ShapeDtypeStruct((t, width), F32), st, st],
        scratch_shapes=[pltpu.VMEM((t, gp), F32), pltpu.VMEM((t, gp), F32)],
        compiler_params=_params(1),
        name="s5_sample",
    )(u, h0_re, h0_im, *weights)


def _out_proj_kernel(x_ref, attn_ref, ssm_ref, ga_ref, gs_ref, wa_ref, ws_ref, gp_ref, o_ref):
    na = _rms(attn_ref[...], ga_ref[...]).astype(BF16)
    ns = _rms(ssm_ref[...], gs_ref[...]).astype(BF16)
    mix = _dot(na, wa_ref[...]) + _dot(ns, ws_ref[...])
    o_ref[...] = x_ref[...] + _rms(mix, gp_ref[...])


def _out_proj(x, attn, ssm, ga, gs, w_out, gp, *, layer, tm):
    t, d = x.shape
    aw = attn.shape[1]
    sw = ssm.shape[1]
    assert aw == sw and w_out.shape[1] == aw + sw
    row = lambda i: (i, 0)
    return pl.pallas_call(
        _out_proj_kernel,
        grid=(t // tm,),
        in_specs=[pl.BlockSpec((tm, d), row), pl.BlockSpec((tm, aw), row), pl.BlockSpec((tm, sw), row),
                  _layer_spec(ga, layer), _layer_spec(gs, layer),
                  pl.BlockSpec((None, aw, d), lambda i: (layer, 0, 0)),
                  pl.BlockSpec((None, sw, d), lambda i: (layer, 1, 0)),
                  _layer_spec(gp, layer)],
        out_specs=pl.BlockSpec((tm, d), row),
        out_shape=jax.ShapeDtypeStruct((t, d), F32),
        compiler_params=_params(1),
        name="out_proj",
    )(x, attn, ssm, ga, gs, w_out, w_out, gp)


def _ffn_kernel(x_ref, g1_ref, wup_ref, wdn_ref, g2_ref, o_ref, hn_ref, acc_ref):
    f = pl.program_id(1)

    @pl.when(f == 0)
    def _():
        hn_ref[...] = _rms(x_ref[...], g1_ref[...]).astype(BF16)
        acc_ref[...] = jnp.zeros(acc_ref.shape, F32)

    a = jnp.square(jnp.maximum(_dot(hn_ref[...], wup_ref[...]), 0.0)).astype(BF16)
    acc_ref[...] += _dot(a, wdn_ref[...])

    @pl.when(f == pl.num_programs(1) - 1)
    def _():
        o_ref[...] = x_ref[...] + _rms(acc_ref[...], g2_ref[...])


def _ffn(x, g1, wup, wdn, g2, *, layer, tm, tf):
    t, d = x.shape
    d_ff = wup.shape[2]
    row = lambda i, f: (i, 0)
    return pl.pallas_call(
        _ffn_kernel,
        grid=(t // tm, d_ff // tf),
        in_specs=[pl.BlockSpec((tm, d), row), _layer_spec(g1, layer),
                  pl.BlockSpec((None, d, tf), lambda i, f: (layer, 0, f)),
                  pl.BlockSpec((None, tf, d), lambda i, f: (layer, f, 0)),
                  _layer_spec(g2, layer)],
        out_specs=pl.BlockSpec((tm, d), row),
        out_shape=jax.ShapeDtypeStruct((t, d), F32),
        scratch_shapes=[pltpu.VMEM((tm, d), BF16), pltpu.VMEM((tm, d), F32)],
        compiler_params=_params(2),
        name="ffn",
    )(x, g1, wup, wdn, g2)


def _rope_tables(pos, rope):
    half = rope // 2
    inv_freq = jnp.exp(-math.log(ROPE_THETA) * 2.0 * jnp.arange(half, dtype=F32) / rope)
    ang = pos.astype(F32)[:, None] * inv_freq[None, :]
    cos, sin = jnp.cos(ang), jnp.sin(ang)
    pad = jnp.zeros((pos.shape[0], LANE - rope), F32)
    return jnp.concatenate([cos, cos, pad], axis=1), jnp.concatenate([-sin, sin, pad], axis=1)


def _swap_halves(w):
    half = w.shape[-1] // 2
    return jnp.concatenate([w[..., half:], w[..., :half]], axis=-1)


def _pad_lanes(w):
    return jnp.pad(w, [(0, 0)] * (w.ndim - 1) + [(0, LANE - w.shape[-1])])


def _tile(n, pref):
    return pref if n % pref == 0 else n


def kernel(x_prompt, x_sample, cache_kv_latent, cache_k_rope, state_ssm_re, state_ssm_im, page_table,
           g_pre_mix, w_in, g_q_lat, w_uq, g_kv_lat, w_uk, w_uv, lam_re, lam_im, log_dt, b_re, b_im,
           c_re, c_im, d_skip, w_glu, g_attn_out, g_ssm_out, w_out, g_post_mix, g_pre_mlp, w_up,
           w_down, g_post_mlp):
    bsz, seq, d = x_prompt.shape
    n_seq, dec_seq, _ = x_sample.shape
    assert dec_seq == 1
    depth = w_in.shape[0]
    q_lora = g_q_lat.shape[1]
    kv_lora, n_heads, nope = w_uk.shape[1:]
    v_dim = w_uv.shape[3]
    rope = cache_k_rope.shape[3]
    ssm_w = g_ssm_out.shape[1]
    n_grp, n_state = lam_re.shape[1:]
    gp = n_grp * n_state
    page = cache_kv_latent.shape[2]
    past_len = page_table.shape[1] * page
    assert nope == LANE and v_dim == LANE and rope * 2 == LANE
    scale = (nope + rope) ** -0.5
    seg_len = seq // N_SEG

    cos_p, sin_p = _rope_tables(jnp.arange(seq, dtype=jnp.int32), rope)
    cos_s, sin_s = _rope_tables(jnp.full((n_seq,), past_len, jnp.int32), rope)
    cache_rt = jnp.swapaxes(cache_k_rope, 2, 3)

    o_r, o_u = q_lora + kv_lora, q_lora + kv_lora + rope
    w_kr = w_in[:, :, o_r:o_u]
    w_in_b = jnp.concatenate(
        [w_in[:, :, :o_r], w_in[:, :, o_u:], _pad_lanes(w_kr), _pad_lanes(_swap_halves(w_kr))],
        axis=2).astype(BF16)
    wq = w_uq.reshape(depth, q_lora, n_heads, nope + rope)
    wq_r = wq[..., nope:]
    w_uq_b = jnp.concatenate(
        [wq[..., :nope].reshape(depth, q_lora, -1), _pad_lanes(wq_r).reshape(depth, q_lora, -1),
         _pad_lanes(_swap_halves(wq_r)).reshape(depth, q_lora, -1)], axis=2).astype(BF16)
    w_uk_b = w_uk.reshape(depth, kv_lora, n_heads * nope).astype(BF16)
    w_uv_b = w_uv.reshape(depth, kv_lora, n_heads * v_dim).astype(BF16)
    w_glu_b = w_glu.astype(BF16)
    w_out_b = w_out.astype(BF16)
    w_up_b = w_up.astype(BF16)
    w_dn_b = w_down.astype(BF16)
    gain = lambda a: a.reshape(depth, 1, -1)
    g_pre_mix, g_q_lat, g_kv_lat, g_attn_out, g_ssm_out, g_post_mix, g_pre_mlp, g_post_mlp = map(
        gain, (g_pre_mix, g_q_lat, g_kv_lat, g_attn_out, g_ssm_out, g_post_mix, g_pre_mlp, g_post_mlp))
    d_all = gain(d_skip)

    ab_re, ab_im, ap_re, ap_im, bb_re, bb_im = _s5_disc(lam_re, lam_im, log_dt, b_re, b_im, seg_len=seg_len)
    wbr, wbi, cdr, cdi = _s5_dense(bb_re, bb_im, c_re, c_im)
    a_re = jnp.broadcast_to(ab_re.reshape(depth, 1, gp), (depth, N_SEG, gp))
    a_im = jnp.broadcast_to(ab_im.reshape(depth, 1, gp), (depth, N_SEG, gp))
    ap_re = ap_re.reshape(depth, 1, gp)
    ap_im = ap_im.reshape(depth, 1, gp)
    h0_re = state_ssm_re.reshape(depth, n_seq, gp)
    h0_im = state_ssm_im.reshape(depth, n_seq, gp)

    tm_p = _tile(seq, 512)
    tf = _tile(w_up.shape[2], 1024)
    xp = x_prompt.reshape(bsz * seq, d)
    xs = x_sample.reshape(n_seq, d)
    in_kw = dict(q_lora=q_lora, kv_lora=kv_lora, ssm_w=ssm_w, rope=rope)

    outs = [[] for _ in range(8)]
    for l in range(depth):
        qlat, ckv, ckvb, kr, krb, u = _in_proj(
            xp, g_pre_mix, w_in_b, g_q_lat, g_kv_lat, cos_p, sin_p, layer=l, tm=tm_p, **in_kw)
        q, k, v = _qkv_prompt(qlat, ckvb, krb, w_uq_b, w_uk_b, w_uv_b, cos_p, sin_p,
                              layer=l, tm=tm_p, n_heads=n_heads, scale=scale)
        attn = _flash(q, k, v, bsz=bsz, seq=seq, n_heads=n_heads, blk=_tile(seq, 512))
        ssm, hf_r, hf_i = _s5_prompt(u, wbr, wbi, a_re, a_im, ap_re, ap_im, cdr, cdi, d_all, w_glu_b,
                                     layer=l, bsz=bsz, seq=seq, sb=_tile(seg_len, 32))
        x1 = _out_proj(xp, attn, ssm, g_attn_out, g_ssm_out, w_out_b, g_post_mix, layer=l, tm=tm_p)
        xp = _ffn(x1, g_pre_mlp, w_up_b, w_dn_b, g_post_mlp, layer=l, tm=tm_p, tf=tf)
        outs[0].append(ckv.reshape(bsz, seq, kv_lora))
        outs[1].append(kr.reshape(bsz, seq, rope))
        outs[2].append(hf_r.reshape(bsz, n_grp, n_state))
        outs[3].append(hf_i.reshape(bsz, n_grp, n_state))

        qlat, ckv, ckvb, kr, krb, u = _in_proj(
            xs, g_pre_mix, w_in_b, g_q_lat, g_kv_lat, cos_s, sin_s, layer=l, tm=n_seq, **in_kw)
        qs = _q_sample(qlat, w_uq_b, w_uk_b, cos_s, sin_s, layer=l, n_heads=n_heads, scale=scale)
        o_lat = _paged(page_table, qs.reshape(n_seq, n_heads, kv_lora + LANE),
                       ckv.reshape(n_seq, 1, kv_lora), _pad_lanes(kr).reshape(n_seq, 1, LANE),
                       cache_kv_latent, cache_rt, layer=l, n_heads=n_heads)
        attn = _uv_proj(o_lat.reshape(n_seq, n_heads * kv_lora), w_uv_b, layer=l, n_heads=n_heads)
        ssm, h_r, h_i = _s5_sample(u, h0_re, h0_im, wbr, wbi, a_re, a_im, cdr, cdi, d_all, w_glu_b, layer=l)
        x1 = _out_proj(xs, attn, ssm, g_attn_out, g_ssm_out, w_out_b, g_post_mix, layer=l, tm=n_seq)
        xs = _ffn(x1, g_pre_mlp, w_up_b, w_dn_b, g_post_mlp, layer=l, tm=n_seq, tf=tf)
        outs[4].append(ckv.reshape(n_seq, 1, kv_lora))
        outs[5].append(kr.reshape(n_seq, 1, rope))
        outs[6].append(h_r.reshape(n_seq, n_grp, n_state))
        outs[7].append(h_i.reshape(n_seq, n_grp, n_state))

    return (xp.reshape(bsz, seq, d), xs.reshape(n_seq, 1, d)) + tuple(jnp.stack(o) for o in outs)
```
